```python
import math
import jax, jax.numpy as jnp
from jax import lax
import numpy as np

D_MODEL = 1024
BATCH = 32
SEQ = 2048
DEPTH = 1

GLA_HEADS = 4
GLA_DK = 64
GLA_DV = 128
GLA_RANK = 16
GLA_CHUNK = 64
GLA_GATE_NORM = 16.0
SWA_HEADS = 8
SWA_KV_HEADS = 2
SWA_HD = 64
SWA_WINDOW = 128
SWA_BLOCK = 128
N_BRANCH = 2

GLA_QK_W = GLA_HEADS * GLA_DK
GLA_V_W = GLA_HEADS * GLA_DV
SWA_Q_W = SWA_HEADS * SWA_HD
SWA_KV_W = SWA_KV_HEADS * SWA_HD
IN_SPLITS = (GLA_QK_W, GLA_QK_W, GLA_V_W, GLA_RANK, GLA_V_W, SWA_Q_W, SWA_KV_W, SWA_KV_W, SWA_Q_W, N_BRANCH * D_MODEL)
D_IN = 4880
DEEPNORM_ALPHA = (2.0 * DEPTH) ** 0.25
DEEPNORM_BETA = (8.0 * DEPTH) ** -0.25
LN_EPS = 1e-5
RMS_EPS = 1e-6

kernel_name = "hybrid_gla_swa_sink_alibi_deepnorm"


def _alibi_slopes(n):
    return 2.0 ** (-8.0 * jnp.arange(1, n + 1, dtype=jnp.float32) / n)


def _gla(q, k, v, log_g):
    B, S, H, DK = q.shape
    DV = v.shape[-1]
    L = GLA_CHUNK
    C = S // L
    q = q.reshape(B, C, L, H, DK) * (DK ** -0.5)
    k = k.reshape(B, C, L, H, DK)
    v = v.reshape(B, C, L, H, DV)
    G = lax.cumsum(log_g.reshape(B, C, L, H, DK), axis=2)
    G_last = G[:, :, -1]
    q_dec = q * jnp.exp(G)
    k_inv = k * jnp.exp(-G)
    causal = jnp.tril(jnp.ones((L, L), dtype=bool))
    A = jnp.einsum('bclhd,bcmhd->bchlm', q_dec, k_inv)
    A = jnp.where(causal, A, 0.0)
    o_intra = jnp.einsum('bchlm,bcmhv->bclhv', A, v)
    k_to_end = k * jnp.exp(G_last[:, :, None] - G)
    kv = jnp.einsum('bclhd,bclhv->cbhdv', k_to_end, v)
    decay = jnp.exp(G_last).transpose(1, 0, 2, 3)

    def step(state, inp):
        d, kv_c = inp
        return state * d[..., None] + kv_c, state

    s0 = jnp.zeros((B, H, DK, DV), dtype=q.dtype)
    _, s_prev = lax.scan(step, s0, (decay, kv))
    o_inter = jnp.einsum('bclhd,cbhdv->bclhv', q_dec, s_prev)
    return (o_intra + o_inter).reshape(B, S, H, DV)


def _swa_sink_alibi(q, k, v, sinks):
    B, S, H, D = q.shape
    KV = k.shape[2]
    G = H // KV
    W = SWA_BLOCK
    N = S // W
    qb = q.reshape(B, N, W, KV, G, D)
    pad = ((0, 0), (W, 0), (0, 0), (0, 0))
    kp = jnp.pad(k, pad).reshape(B, N + 1, W, KV, D)
    vp = jnp.pad(v, pad).reshape(B, N + 1, W, KV, D)
    kb = jnp.concatenate([kp[:, :-1], kp[:, 1:]], axis=2)
    vb = jnp.concatenate([vp[:, :-1], vp[:, 1:]], axis=2)
    scores = jnp.einsum('bnqkgd,bnskd->bnkgqs', qb, kb).astype(jnp.float32) * (D ** -0.5)
    q_pos = jnp.arange(N)[:, None] * W + jnp.arange(W)[None, :]
    k_pos = (jnp.arange(N)[:, None] - 1) * W + jnp.arange(2 * W)[None, :]
    dist = q_pos[:, :, None] - k_pos[:, None, :]
    valid = (dist >= 0) & (dist < SWA_WINDOW) & (k_pos[:, None, :] >= 0)
    slopes = _alibi_slopes(H).reshape(KV, G)
    scores = scores - slopes[None, None, :, :, None, None] * dist.astype(jnp.float32)[None, :, None, None]
    scores = jnp.where(valid[None, :, None, None], scores, -jnp.inf)
    sink = jnp.broadcast_to(sinks.astype(jnp.float32).reshape(1, 1, KV, G, 1, 1), scores.shape[:-1] + (1,))
    probs = jax.nn.softmax(jnp.concatenate([scores, sink], axis=-1), axis=-1)[..., :-1]
    out = jnp.einsum('bnkgqs,bnskd->bnqkgd', probs.astype(v.dtype), vb)
    return out.reshape(B, S, H * D)


def _mixer(x, w_in, w_gk2, b_gk, gla_norm_g, w_o_gla, sinks, w_o_swa, b_gate, w_out):
    B, S, _ = x.shape
    h = x @ w_in
    idx = tuple(int(i) for i in np.cumsum(IN_SPLITS)[:-1])
    q_g, k_g, v_g, gk_lr, z_g, q_s, k_s, v_s, z_s, gate_logits = jnp.split(h, idx, axis=-1)

    log_g = jax.nn.log_sigmoid((gk_lr @ w_gk2 + b_gk).astype(jnp.float32)) / GLA_GATE_NORM
    o = _gla(q_g.astype(jnp.float32).reshape(B, S, GLA_HEADS, GLA_DK),
             k_g.astype(jnp.float32).reshape(B, S, GLA_HEADS, GLA_DK),
             v_g.astype(jnp.float32).reshape(B, S, GLA_HEADS, GLA_DV),
             log_g.reshape(B, S, GLA_HEADS, GLA_DK))
    o = o * lax.rsqrt(jnp.mean(o * o, axis=-1, keepdims=True) + RMS_EPS) * gla_norm_g.astype(jnp.float32)
    y_gla = (o.reshape(B, S, GLA_V_W).astype(x.dtype) * jax.nn.silu(z_g)) @ w_o_gla

    o_s = _swa_sink_alibi(q_s.reshape(B, S, SWA_HEADS, SWA_HD),
                          k_s.reshape(B, S, SWA_KV_HEADS, SWA_HD),
                          v_s.reshape(B, S, SWA_KV_HEADS, SWA_HD), sinks)
    y_swa = (o_s * jax.nn.silu(z_s)) @ w_o_swa

    gates = jax.nn.sigmoid(gate_logits + b_gate)
    g_a, g_b = jnp.split(gates, 2, axis=-1)
    return (g_a * y_gla + g_b * y_swa) @ w_out


def setup_inputs(seed: int = 0) -> dict:
    key = jax.random.key(seed)
    ks = jax.random.split(key, 12)
    beta = DEEPNORM_BETA
    col_scales = (1.0, 1.0, beta, 1.0, 1.0, 1.0, 1.0, beta, 1.0, 1.0)
    col_scale = jnp.concatenate([jnp.full((n,), s, dtype=jnp.float32) for n, s in zip(IN_SPLITS, col_scales)])
    x = jax.random.normal(ks[0], (BATCH, SEQ, D_MODEL), jnp.float32)
    w_in = jax.random.normal(ks[1], (D_MODEL, D_IN), jnp.float32) * (D_MODEL ** -0.5) * col_scale
    w_gk2 = jax.random.normal(ks[2], (GLA_RANK, GLA_QK_W), jnp.float32) * (GLA_RANK ** -0.5)
    b_gk = 0.01 * jax.random.normal(ks[3], (GLA_QK_W,), jnp.float32)
    gla_norm_g = 1.0 + 0.02 * jax.random.normal(ks[4], (GLA_DV,), jnp.float32)
    w_o_gla = jax.random.normal(ks[5], (GLA_V_W, D_MODEL), jnp.float32) * (GLA_V_W ** -0.5) * beta
    sinks = 0.5 * jax.random.normal(ks[6], (SWA_HEADS,), jnp.float32)
    w_o_swa = jax.random.normal(ks[7], (SWA_Q_W, D_MODEL), jnp.float32) * (SWA_Q_W ** -0.5) * beta
    b_gate = 0.01 * jax.random.normal(ks[8], (N_BRANCH * D_MODEL,), jnp.float32)
    w_out = jax.random.normal(ks[9], (D_MODEL, D_MODEL), jnp.float32) * (D_MODEL ** -0.5) * beta
    ln_g = 1.0 + 0.02 * jax.random.normal(ks[10], (D_MODEL,), jnp.float32)
    ln_b = 0.02 * jax.random.normal(ks[11], (D_MODEL,), jnp.float32)
    return {"x": x, "w_in": w_in, "w_gk2": w_gk2, "b_gk": b_gk, "gla_norm_g": gla_norm_g,
            "w_o_gla": w_o_gla, "sinks": sinks, "w_o_swa": w_o_swa, "b_gate": b_gate,
            "w_out": w_out, "ln_g": ln_g, "ln_b": ln_b}


def reference(x, w_in, w_gk2, b_gk, gla_norm_g, w_o_gla, sinks, w_o_swa, b_gate, w_out, ln_g, ln_b):
    for _ in range(DEPTH):
        y = _mixer(x, w_in, w_gk2, b_gk, gla_norm_g, w_o_gla, sinks, w_o_swa, b_gate, w_out)
        r = (DEEPNORM_ALPHA * x + y).astype(jnp.float32)
        mu = jnp.mean(r, axis=-1, keepdims=True)
        var = jnp.mean(jnp.square(r - mu), axis=-1, keepdims=True)
        x = ((r - mu) * lax.rsqrt(var + LN_EPS) * ln_g.astype(jnp.float32) + ln_b.astype(jnp.float32)).astype(x.dtype)
    return x
```

```python
import functools
import math

import jax
import jax.numpy as jnp
from jax import lax
from jax.experimental import pallas as pl
from jax.experimental.pallas import tpu as pltpu

F32 = jnp.float32
BF16 = jnp.bfloat16

D_MODEL = 1024
DEPTH = 1
GLA_HEADS = 4
GLA_DK = 64
GLA_DV = 128
GLA_RANK = 16
GLA_CHUNK = 64
GLA_GATE_NORM = 16.0
SWA_HEADS = 8
SWA_KV_HEADS = 2
SWA_HD = 64
SWA_WINDOW = 128
SWA_BLOCK = 128
GLA_QK_W = GLA_HEADS * GLA_DK
GLA_V_W = GLA_HEADS * GLA_DV
SWA_Q_W = SWA_HEADS * SWA_HD
SWA_KV_W = SWA_KV_HEADS * SWA_HD
DEEPNORM_ALPHA = (2.0 * DEPTH) ** 0.25
LN_EPS = 1e-5
RMS_EPS = 1e-6

LANES = 128
RANK_PAD = LANES
VMEM_LIMIT_BYTES = 56 * 1024 * 1024

GLA_COLS = 2 * GLA_QK_W + 2 * GLA_V_W + RANK_PAD
SWA_COLS = 2 * SWA_Q_W + 2 * SWA_KV_W
GATE_COLS = 2 * D_MODEL
PROJ_COLS = GLA_COLS + SWA_COLS + GATE_COLS

PROJ_ROWS = 512
PROJ_COL_CHUNK = 512
SEQ_TILE = 256
OUT_ROWS = 256


def _sigmoid(v):
    return 1.0 / (1.0 + jnp.exp(-v))


def _log_sigmoid(v):
    return jnp.minimum(v, 0.0) - jnp.log1p(jnp.exp(-jnp.abs(v)))


def _split2(a):
    hi = a.astype(BF16)
    lo = (a - hi.astype(F32)).astype(BF16)
    return hi, lo


def _proj_kernel(x_ref, w_ref, gla_ref, swa_ref, gate_ref):
    xb = x_ref[...].astype(BF16)
    for out_ref, base in ((gla_ref, 0), (swa_ref, GLA_COLS), (gate_ref, GLA_COLS + SWA_COLS)):
        width = out_ref.shape[1]
        for c0 in range(0, width, PROJ_COL_CHUNK):
            c1 = min(c0 + PROJ_COL_CHUNK, width)
            acc = jnp.dot(xb, w_ref[:, base + c0:base + c1], preferred_element_type=F32)
            out_ref[:, c0:c1] = acc.astype(BF16)


def _project(x2d, w_packed):
    t = x2d.shape[0]
    assert t % PROJ_ROWS == 0
    return pl.pallas_call(
        _proj_kernel,
        grid=(t // PROJ_ROWS,),
        in_specs=[
            pl.BlockSpec((PROJ_ROWS, D_MODEL), lambda i: (i, 0)),
            pl.BlockSpec((D_MODEL, PROJ_COLS), lambda i: (0, 0)),
        ],
        out_specs=[
            pl.BlockSpec((PROJ_ROWS, GLA_COLS), lambda i: (i, 0)),
            pl.BlockSpec((PROJ_ROWS, SWA_COLS), lambda i: (i, 0)),
            pl.BlockSpec((PROJ_ROWS, GATE_COLS), lambda i: (i, 0)),
        ],
        out_shape=[
            jax.ShapeDtypeStruct((t, GLA_COLS), BF16),
            jax.ShapeDtypeStruct((t, SWA_COLS), BF16),
            jax.ShapeDtypeStruct((t, GATE_COLS), BF16),
        ],
        compiler_params=pltpu.CompilerParams(
            dimension_semantics=("parallel",), vmem_limit_bytes=VMEM_LIMIT_BYTES),
        name="in_proj",
    )(x2d, w_packed)


def _gla_kernel(h_ref, wgk_ref, bgk_ref, ng_ref, o_ref, state_ref):
    ts = h_ref.shape[0]
    n_chunks = ts // GLA_CHUNK

    @pl.when(pl.program_id(1) == 0)
    def _():
        state_ref[...] = jnp.zeros_like(state_ref)

    lr = h_ref[:, 2 * GLA_QK_W + 2 * GLA_V_W:GLA_COLS]
    gk = jnp.dot(lr, wgk_ref[...], preferred_element_type=F32) + bgk_ref[...]
    log_g = _log_sigmoid(gk) * (1.0 / GLA_GATE_NORM)

    row = lax.broadcasted_iota(jnp.int32, (ts, ts), 0)
    col = lax.broadcasted_iota(jnp.int32, (ts, ts), 1)
    same_chunk = (row // GLA_CHUNK) == (col // GLA_CHUNK)
    lower = jnp.where(same_chunk & (col <= row), 1.0, 0.0).astype(BF16)
    upper = jnp.where(same_chunk & (col > row), 1.0, 0.0).astype(BF16)
    lg_hi, lg_lo = _split2(log_g)
    g_cum = (jnp.dot(lower, lg_hi, preferred_element_type=F32)
             + jnp.dot(lower, lg_lo, preferred_element_type=F32))
    g_rev = (jnp.dot(upper, lg_hi, preferred_element_type=F32)
             + jnp.dot(upper, lg_lo, preferred_element_type=F32))
    g_cum_t = g_cum.T

    r_k = lax.broadcasted_iota(jnp.int32, (GLA_QK_W, GLA_QK_W), 0)
    c_k = lax.broadcasted_iota(jnp.int32, (GLA_QK_W, GLA_QK_W), 1)
    k_head_mask = (r_k // GLA_CHUNK) == (c_k // GLA_DK)
    r_v = lax.broadcasted_iota(jnp.int32, (GLA_QK_W, GLA_V_W), 0)
    c_v = lax.broadcasted_iota(jnp.int32, (GLA_QK_W, GLA_V_W), 1)
    v_head_mask = (r_v // GLA_DK) == (c_v // GLA_DV)
    r_a = lax.broadcasted_iota(jnp.int32, (GLA_CHUNK, GLA_QK_W), 0)
    c_a = lax.broadcasted_iota(jnp.int32, (GLA_CHUNK, GLA_QK_W), 1)
    causal = (c_a % GLA_CHUNK) <= r_a

    o_chunks = []
    for c in range(n_chunks):
        r0 = c * GLA_CHUNK
        r1 = r0 + GLA_CHUNK
        g_c = g_cum[r0:r1]
        q_c = h_ref[r0:r1, 0:GLA_QK_W].astype(F32)
        k_c = h_ref[r0:r1, GLA_QK_W:2 * GLA_QK_W].astype(F32)
        v_c = h_ref[r0:r1, 2 * GLA_QK_W:2 * GLA_QK_W + GLA_V_W]
        q_dec = (q_c * ((GLA_DK ** -0.5) * jnp.exp(g_c))).astype(BF16)
        k_inv = (k_c * jnp.exp(-g_c)).astype(BF16)
        k_end = (k_c * jnp.exp(g_rev[r0:r1])).astype(BF16)

        k_bd = jnp.where(k_head_mask, jnp.concatenate([k_inv] * GLA_HEADS, axis=0), jnp.zeros((), BF16))
        a = lax.dot_general(q_dec, k_bd, (((1,), (1,)), ((), ())), preferred_element_type=F32)
        a = jnp.where(causal, a, 0.0).astype(BF16)
        v_bd = jnp.where(v_head_mask, jnp.concatenate([v_c] * GLA_HEADS, axis=0), jnp.zeros((), BF16))
        state = state_ref[...]
        o_c = (jnp.dot(a, v_bd, preferred_element_type=F32)
               + jnp.dot(q_dec, state.astype(BF16), preferred_element_type=F32))
        o_chunks.append(o_c)

        kv = lax.dot_general(k_end, v_c, (((0,), (0,)), ((), ())), preferred_element_type=F32)
        decay = jnp.exp(g_cum_t[:, r1 - 1:r1])
        state_ref[...] = state * decay + jnp.where(v_head_mask, kv, 0.0)

    o = jnp.concatenate(o_chunks, axis=0)
    normed = []
    for hd in range(GLA_HEADS):
        o_h = o[:, hd * GLA_DV:(hd + 1) * GLA_DV]
        ms = jnp.mean(o_h * o_h, axis=-1, keepdims=True)
        normed.append(o_h * lax.rsqrt(ms + RMS_EPS) * ng_ref[...])
    o_n = jnp.concatenate(normed, axis=1)
    z = h_ref[:, 2 * GLA_QK_W + GLA_V_W:2 * GLA_QK_W + 2 * GLA_V_W].astype(F32)
    o_ref[...] = (o_n * (z * _sigmoid(z))).astype(BF16)


def _gla(h_gla, w_gk2p, b_gk, norm_g, batch, seq):
    assert seq % SEQ_TILE == 0 and SEQ_TILE % GLA_CHUNK == 0
    steps = seq // SEQ_TILE
    return pl.pallas_call(
        _gla_kernel,
        grid=(batch, steps),
        in_specs=[
            pl.BlockSpec((SEQ_TILE, GLA_COLS), lambda b, s: (b * steps + s, 0)),
            pl.BlockSpec((RANK_PAD, GLA_QK_W), lambda b, s: (0, 0)),
            pl.BlockSpec((1, GLA_QK_W), lambda b, s: (0, 0)),
            pl.BlockSpec((1, GLA_DV), lambda b, s: (0, 0)),
        ],
        out_specs=pl.BlockSpec((SEQ_TILE, GLA_V_W), lambda b, s: (b * steps + s, 0)),
        out_shape=jax.ShapeDtypeStruct((batch * seq, GLA_V_W), BF16),
        scratch_shapes=[pltpu.VMEM((GLA_QK_W, GLA_V_W), F32)],
        compiler_params=pltpu.CompilerParams(
            dimension_semantics=("arbitrary", "arbitrary"), vmem_limit_bytes=VMEM_LIMIT_BYTES),
        name="gla",
    )(h_gla, w_gk2p, b_gk, norm_g)


def _swa_bias_table():
    w = SWA_BLOCK
    slopes = 2.0 ** (-8.0 * jnp.arange(1, SWA_HEADS + 1, dtype=F32) / SWA_HEADS)
    q_idx = jnp.arange(w)[:, None]
    k_idx = jnp.arange(2 * w)[None, :]
    dist = q_idx - (k_idx - w)
    valid = (dist >= 0) & (dist < SWA_WINDOW)
    bias = -slopes[:, None, None] * dist.astype(F32)[None]
    rest = jnp.where(valid[None], bias, -jnp.inf)
    first = jnp.where((valid & (k_idx >= w))[None], bias, -jnp.inf)
    return jnp.stack([first, rest]).astype(F32)


def _swa_kernel(h_ref, bias_ref, sink_ref, o_ref, kprev_ref, vprev_ref):
    ts = h_ref.shape[0]
    w = SWA_BLOCK
    n_blocks = ts // w
    step = pl.program_id(1)
    group = SWA_HEADS // SWA_KV_HEADS
    pairs_per_kv = group // 2
    k_off = SWA_Q_W
    v_off = SWA_Q_W + SWA_KV_W
    z_off = SWA_Q_W + 2 * SWA_KV_W

    @pl.when(step == 0)
    def _():
        kprev_ref[...] = jnp.zeros_like(kprev_ref)
        vprev_ref[...] = jnp.zeros_like(vprev_ref)

    lane = lax.broadcasted_iota(jnp.int32, (2 * w, SWA_KV_W), 1)
    low = lane < SWA_HD

    for n in range(n_blocks):
        r0 = n * w
        r1 = r0 + w
        if n == 0:
            k_prev = kprev_ref[...]
            v_prev = vprev_ref[...]
            bias_idx = jnp.where(step == 0, 0, 1)
        else:
            k_prev = h_ref[r0 - w:r0, k_off:k_off + SWA_KV_W]
            v_prev = h_ref[r0 - w:r0, v_off:v_off + SWA_KV_W]
            bias_idx = 1
        k_cat = jnp.concatenate([k_prev, h_ref[r0:r1, k_off:k_off + SWA_KV_W]], axis=0).astype(F32)
        v_cat = jnp.concatenate([v_prev, h_ref[r0:r1, v_off:v_off + SWA_KV_W]], axis=0).astype(F32)
        k_swap = pltpu.roll(k_cat, SWA_HD, 1)
        v_swap = pltpu.roll(v_cat, SWA_HD, 1)

        outs = []
        for g in range(SWA_KV_HEADS):
            src_low, src_high = (k_cat, k_swap) if g == 0 else (k_swap, k_cat)
            k_lo = jnp.where(low, src_low, 0.0).astype(BF16)
            k_hi = jnp.where(low, 0.0, src_high).astype(BF16)
            src_low, src_high = (v_cat, v_swap) if g == 0 else (v_swap, v_cat)
            v_lo = jnp.where(low, src_low, 0.0).astype(BF16)
            v_hi = jnp.where(low, 0.0, src_high).astype(BF16)

            q_st = jnp.concatenate(
                [h_ref[r0:r1, (g * pairs_per_kv + p) * LANES:(g * pairs_per_kv + p + 1) * LANES]
                 for p in range(pairs_per_kv)], axis=0)
            probs = []
            for parity, k_side in ((0, k_lo), (1, k_hi)):
                s = lax.dot_general(q_st, k_side, (((1,), (1,)), ((), ())), preferred_element_type=F32)
                s = s * (SWA_HD ** -0.5)
                p_rows = []
                for p in range(pairs_per_kv):
                    head = g * group + 2 * p + parity
                    sc = s[p * w:(p + 1) * w] + bias_ref[bias_idx, head]
                    sink = sink_ref[head]
                    m = jnp.maximum(jnp.max(sc, axis=-1, keepdims=True), sink)
                    e = jnp.exp(sc - m)
                    denom = jnp.sum(e, axis=-1, keepdims=True) + jnp.exp(sink - m)
                    p_rows.append((e / denom).astype(BF16))
                probs.append(jnp.concatenate(p_rows, axis=0))
            outs.append(jnp.dot(probs[0], v_lo, preferred_element_type=F32)
                        + jnp.dot(probs[1], v_hi, preferred_element_type=F32))

        o_s = jnp.concatenate(
            [outs[g][p * w:(p + 1) * w] for g in range(SWA_KV_HEADS) for p in range(pairs_per_kv)], axis=1)
        z = h_ref[r0:r1, z_off:z_off + SWA_Q_W].astype(F32)
        o_ref[r0:r1, :] = (o_s * (z * _sigmoid(z))).astype(BF16)

    kprev_ref[...] = h_ref[ts - w:ts, k_off:k_off + SWA_KV_W]
    vprev_ref[...] = h_ref[ts - w:ts, v_off:v_off + SWA_KV_W]


def _swa(h_swa, bias, sinks, batch, seq):
    assert seq % SEQ_TILE == 0 and SEQ_TILE % SWA_BLOCK == 0
    steps = seq // SEQ_TILE
    return pl.pallas_call(
        _swa_kernel,
        grid=(batch, steps),
        in_specs=[
            pl.BlockSpec((SEQ_TILE, SWA_COLS), lambda b, s: (b * steps + s, 0)),
            pl.BlockSpec((2, SWA_HEADS, SWA_BLOCK, 2 * SWA_BLOCK), lambda b, s: (0, 0, 0, 0)),
            pl.BlockSpec(memory_space=pltpu.SMEM),
        ],
        out_specs=pl.BlockSpec((SEQ_TILE, SWA_Q_W), lambda b, s: (b * steps + s, 0)),
        out_shape=jax.ShapeDtypeStruct((batch * seq, SWA_Q_W), BF16),
        scratch_shapes=[pltpu.VMEM((SWA_BLOCK, SWA_KV_W), BF16), pltpu.VMEM((SWA_BLOCK, SWA_KV_W), BF16)],
        compiler_params=pltpu.CompilerParams(
            dimension_semantics=("arbitrary", "arbitrary"), vmem_limit_bytes=VMEM_LIMIT_BYTES),
        name="swa",
    )(h_swa, bias, sinks)


def _out_kernel(x_ref, ag_ref, as_ref, gate_ref, wog_ref, wos_ref, wout_ref, bgate_ref, lng_ref, lnb_ref,
                o_ref):
    y_gla = jnp.dot(ag_ref[...], wog_ref[...], preferred_element_type=F32)
    y_swa = jnp.dot(as_ref[...], wos_ref[...], preferred_element_type=F32)
    gates = _sigmoid(gate_ref[...].astype(F32) + bgate_ref[...])
    merged = gates[:, :D_MODEL] * y_gla + gates[:, D_MODEL:] * y_swa
    y = jnp.dot(merged.astype(BF16), wout_ref[...], preferred_element_type=F32)
    r = DEEPNORM_ALPHA * x_ref[...] + y
    mu = jnp.mean(r, axis=-1, keepdims=True)
    cen = r - mu
    var = jnp.mean(cen * cen, axis=-1, keepdims=True)
    o_ref[...] = cen * lax.rsqrt(var + LN_EPS) * lng_ref[...] + lnb_ref[...]


def _output(x2d, a_gla, a_swa, h_gate, w_o_gla, w_o_swa, w_out, b_gate, ln_g, ln_b):
    t = x2d.shape[0]
    assert t % OUT_ROWS == 0
    rows = lambda width: pl.BlockSpec((OUT_ROWS, width), lambda i: (i, 0))
    whole = lambda shape: pl.BlockSpec(shape, lambda i: (0, 0))
    return pl.pallas_call(
        _out_kernel,
        grid=(t // OUT_ROWS,),
        in_specs=[
            rows(D_MODEL), rows(GLA_V_W), rows(SWA_Q_W), rows(GATE_COLS),
            whole((GLA_V_W, D_MODEL)), whole((SWA_Q_W, D_MODEL)), whole((D_MODEL, D_MODEL)),
            whole((1, GATE_COLS)), whole((1, D_MODEL)), whole((1, D_MODEL)),
        ],
        out_specs=rows(D_MODEL),
        out_shape=jax.ShapeDtypeStruct((t, D_MODEL), F32),
        compiler_params=pltpu.CompilerParams(
            dimension_semantics=("parallel",), vmem_limit_bytes=VMEM_LIMIT_BYTES),
        name="out_stage",
    )(x2d, a_gla, a_swa, h_gate, w_o_gla, w_o_swa, w_out, b_gate, ln_g, ln_b)


def _pack_w_in(w_in):
    edges = [0]
    for n in (GLA_QK_W, GLA_QK_W, GLA_V_W, GLA_RANK, GLA_V_W, SWA_Q_W, SWA_KV_W, SWA_KV_W, SWA_Q_W, GATE_COLS):
        edges.append(edges[-1] + n)
    q_g, k_g, v_g, lr, z_g, q_s, k_s, v_s, z_s, gates = (
        w_in[:, edges[i]:edges[i + 1]] for i in range(10))
    lr_pad = jnp.pad(lr, ((0, 0), (0, RANK_PAD - GLA_RANK)))
    return jnp.concatenate([q_g, k_g, v_g, z_g, lr_pad, q_s, k_s, v_s, z_s, gates], axis=1).astype(BF16)


def kernel(x, w_in, w_gk2, b_gk, gla_norm_g, w_o_gla, sinks, w_o_swa, b_gate, w_out, ln_g, ln_b):
    batch, seq, d_model = x.shape
    assert d_model == D_MODEL and DEPTH == 1
    x2d = x.reshape(batch * seq, d_model)

    w_packed = _pack_w_in(w_in)
    w_gk2p = jnp.pad(w_gk2, ((0, RANK_PAD - GLA_RANK), (0, 0))).astype(BF16)

    h_gla, h_swa, h_gate = _project(x2d, w_packed)
    a_gla = _gla(h_gla, w_gk2p, b_gk.reshape(1, -1).astype(F32), gla_norm_g.reshape(1, -1).astype(F32),
                 batch, seq)
    a_swa = _swa(h_swa, _swa_bias_table(), sinks.astype(F32), batch, seq)
    out = _output(x2d, a_gla, a_swa, h_gate, w_o_gla.astype(BF16), w_o_swa.astype(BF16), w_out.astype(BF16),
                  b_gate.reshape(1, -1).astype(F32), ln_g.reshape(1, -1).astype(F32),
                  ln_b.reshape(1, -1).astype(F32))
    return out.reshape(batch, seq, d_model).astype(x.dtype)
```

```python
import jax
import jax.numpy as jnp
from jax import lax
from jax.experimental import pallas as pl
from jax.experimental.pallas import tpu as pltpu

F32 = jnp.float32
BF16 = jnp.bfloat16

D_MODEL = 1024
DEPTH = 1
GLA_HEADS = 4
GLA_DK = 64
GLA_DV = 128
GLA_RANK = 16
GLA_CHUNK = 64
GLA_GATE_NORM = 16.0
SWA_HEADS = 8
SWA_KV_HEADS = 2
SWA_HD = 64
SWA_WINDOW = 128
SWA_BLOCK = 128
GLA_QK_W = GLA_HEADS * GLA_DK
GLA_V_W = GLA_HEADS * GLA_DV
SWA_Q_W = SWA_HEADS * SWA_HD
SWA_KV_W = SWA_KV_HEADS * SWA_HD
GATE_W = 2 * D_MODEL
DEEPNORM_ALPHA = (2.0 * DEPTH) ** 0.25
LN_EPS = 1e-5
RMS_EPS = 1e-6

LANES = 128
RANK_PAD = LANES
VMEM_LIMIT_BYTES = 56 * 1024 * 1024

MX_V_G = 0
MX_LR = MX_V_G + GLA_V_W
MX_Q_S = MX_LR + RANK_PAD
MX_COLS = MX_Q_S + SWA_Q_W
EW_Q_G = 0
EW_K_G = EW_Q_G + GLA_QK_W
EW_Z_G = EW_K_G + GLA_QK_W
EW_K_S = EW_Z_G + GLA_V_W
EW_V_S = EW_K_S + SWA_KV_W
EW_Z_S = EW_V_S + SWA_KV_W
EW_GATE = EW_Z_S + SWA_Q_W
EW_COLS = EW_GATE + GATE_W
PROJ_COLS = MX_COLS + EW_COLS

PROJ_COL_CHUNK = 512
SEQ_TILE = 256


def _sigmoid(v):
    return 1.0 / (1.0 + jnp.exp(-v))


def _log_sigmoid(v):
    return jnp.minimum(v, 0.0) - jnp.log1p(jnp.exp(-jnp.abs(v)))


def _split2(a):
    hi = a.astype(BF16)
    lo = (a - hi.astype(F32)).astype(BF16)
    return hi, lo


def _project_tile(x_ref, w_ref, mx_ref, ew_ref):
    xb = x_ref[...].astype(BF16)
    for out_ref, base in ((mx_ref, 0), (ew_ref, MX_COLS)):
        width = out_ref.shape[1]
        for c0 in range(0, width, PROJ_COL_CHUNK):
            c1 = min(c0 + PROJ_COL_CHUNK, width)
            acc = jnp.dot(xb, w_ref[:, base + c0:base + c1], preferred_element_type=F32)
            out_ref[:, c0:c1] = acc.astype(out_ref.dtype)


def _gla_tile(mx_ref, ew_ref, wgk_ref, bgk_ref, ng_ref, state_ref):
    ts = mx_ref.shape[0]
    n_chunks = ts // GLA_CHUNK

    gk = jnp.dot(mx_ref[:, MX_LR:MX_LR + RANK_PAD], wgk_ref[...], preferred_element_type=F32) + bgk_ref[...]
    log_g = _log_sigmoid(gk) * (1.0 / GLA_GATE_NORM)

    row = lax.broadcasted_iota(jnp.int32, (ts, ts), 0)
    col = lax.broadcasted_iota(jnp.int32, (ts, ts), 1)
    same_chunk = (row // GLA_CHUNK) == (col // GLA_CHUNK)
    lower = jnp.where(same_chunk & (col <= row), 1.0, 0.0).astype(BF16)
    upper = jnp.where(same_chunk & (col > row), 1.0, 0.0).astype(BF16)
    lg_hi, lg_lo = _split2(log_g)
    g_cum = (jnp.dot(lower, lg_hi, preferred_element_type=F32)
             + jnp.dot(lower, lg_lo, preferred_element_type=F32))
    g_rev = (jnp.dot(upper, lg_hi, preferred_element_type=F32)
             + jnp.dot(upper, lg_lo, preferred_element_type=F32))
    g_cum_t = g_cum.T

    r_k = lax.broadcasted_iota(jnp.int32, (GLA_QK_W, GLA_QK_W), 0)
    c_k = lax.broadcasted_iota(jnp.int32, (GLA_QK_W, GLA_QK_W), 1)
    k_head_mask = (r_k // GLA_CHUNK) == (c_k // GLA_DK)
    r_v = lax.broadcasted_iota(jnp.int32, (GLA_QK_W, GLA_V_W), 0)
    c_v = lax.broadcasted_iota(jnp.int32, (GLA_QK_W, GLA_V_W), 1)
    v_head_mask = (r_v // GLA_DK) == (c_v // GLA_DV)
    r_a = lax.broadcasted_iota(jnp.int32, (GLA_CHUNK, GLA_QK_W), 0)
    c_a = lax.broadcasted_iota(jnp.int32, (GLA_CHUNK, GLA_QK_W), 1)
    causal = (c_a % GLA_CHUNK) <= r_a

    o_chunks = []
    for c in range(n_chunks):
        r0 = c * GLA_CHUNK
        r1 = r0 + GLA_CHUNK
        g_c = g_cum[r0:r1]
        q_c = ew_ref[r0:r1, EW_Q_G:EW_Q_G + GLA_QK_W]
        k_c = ew_ref[r0:r1, EW_K_G:EW_K_G + GLA_QK_W]
        v_c = mx_ref[r0:r1, MX_V_G:MX_V_G + GLA_V_W]
        q_dec = (q_c * ((GLA_DK ** -0.5) * jnp.exp(g_c))).astype(BF16)
        k_inv = (k_c * jnp.exp(-g_c)).astype(BF16)
        k_end = (k_c * jnp.exp(g_rev[r0:r1])).astype(BF16)

        k_bd = jnp.where(k_head_mask, jnp.concatenate([k_inv] * GLA_HEADS, axis=0), jnp.zeros((), BF16))
        a = lax.dot_general(q_dec, k_bd, (((1,), (1,)), ((), ())), preferred_element_type=F32)
        a = jnp.where(causal, a, 0.0).astype(BF16)
        v_bd = jnp.where(v_head_mask, jnp.concatenate([v_c] * GLA_HEADS, axis=0), jnp.zeros((), BF16))
        state = state_ref[...]
        o_c = (jnp.dot(a, v_bd, preferred_element_type=F32)
               + jnp.dot(q_dec, state.astype(BF16), preferred_element_type=F32))
        o_chunks.append(o_c)

        kv = lax.dot_general(k_end, v_c, (((0,), (0,)), ((), ())), preferred_element_type=F32)
        decay = jnp.exp(g_cum_t[:, r1 - 1:r1])
        state_ref[...] = state * decay + jnp.where(v_head_mask, kv, 0.0)

    o = jnp.concatenate(o_chunks, axis=0)
    normed = []
    for hd in range(GLA_HEADS):
        o_h = o[:, hd * GLA_DV:(hd + 1) * GLA_DV]
        ms = jnp.mean(o_h * o_h, axis=-1, keepdims=True)
        normed.append(o_h * lax.rsqrt(ms + RMS_EPS) * ng_ref[...])
    o_n = jnp.concatenate(normed, axis=1)
    z = ew_ref[:, EW_Z_G:EW_Z_G + GLA_V_W]
    return (o_n * (z * _sigmoid(z))).astype(BF16)


def _swa_bias_table():
    w = SWA_BLOCK
    slopes = 2.0 ** (-8.0 * jnp.arange(1, SWA_HEADS + 1, dtype=F32) / SWA_HEADS)
    q_idx = jnp.arange(w)[:, None]
    k_idx = jnp.arange(2 * w)[None, :]
    dist = q_idx - (k_idx - w)
    valid = (dist >= 0) & (dist < SWA_WINDOW)
    bias = -slopes[:, None, None] * dist.astype(F32)[None]
    rest = jnp.where(valid[None], bias, -jnp.inf)
    first = jnp.where((valid & (k_idx >= w))[None], bias, -jnp.inf)
    return jnp.stack([first, rest]).astype(F32)


def _swa_tile(mx_ref, ew_ref, bias_ref, sink_ref, kprev_ref, vprev_ref, first_tile):
    ts = mx_ref.shape[0]
    w = SWA_BLOCK
    n_blocks = ts // w
    group = SWA_HEADS // SWA_KV_HEADS
    pairs_per_kv = group // 2

    lane = lax.broadcasted_iota(jnp.int32, (2 * w, SWA_KV_W), 1)
    low = lane < SWA_HD

    tiles = []
    for n in range(n_blocks):
        r0 = n * w
        r1 = r0 + w
        if n == 0:
            k_prev = kprev_ref[...]
            v_prev = vprev_ref[...]
            bias_idx = jnp.where(first_tile, 0, 1)
        else:
            k_prev = ew_ref[r0 - w:r0, EW_K_S:EW_K_S + SWA_KV_W]
            v_prev = ew_ref[r0 - w:r0, EW_V_S:EW_V_S + SWA_KV_W]
            bias_idx = 1
        k_cat = jnp.concatenate([k_prev, ew_ref[r0:r1, EW_K_S:EW_K_S + SWA_KV_W]], axis=0)
        v_cat = jnp.concatenate([v_prev, ew_ref[r0:r1, EW_V_S:EW_V_S + SWA_KV_W]], axis=0)
        k_swap = pltpu.roll(k_cat, SWA_HD, 1)
        v_swap = pltpu.roll(v_cat, SWA_HD, 1)

        outs = []
        for g in range(SWA_KV_HEADS):
            src_low, src_high = (k_cat, k_swap) if g == 0 else (k_swap, k_cat)
            k_lo = jnp.where(low, src_low, 0.0).astype(BF16)
            k_hi = jnp.where(low, 0.0, src_high).astype(BF16)
            src_low, src_high = (v_cat, v_swap) if g == 0 else (v_swap, v_cat)
            v_lo = jnp.where(low, src_low, 0.0).astype(BF16)
            v_hi = jnp.where(low, 0.0, src_high).astype(BF16)

            q_st = jnp.concatenate(
                [mx_ref[r0:r1, MX_Q_S + (g * pairs_per_kv + p) * LANES:MX_Q_S + (g * pairs_per_kv + p + 1) * LANES]
                 for p in range(pairs_per_kv)], axis=0)
            probs = []
            for parity, k_side in ((0, k_lo), (1, k_hi)):
                s = lax.dot_general(q_st, k_side, (((1,), (1,)), ((), ())), preferred_element_type=F32)
                s = s * (SWA_HD ** -0.5)
                p_rows = []
                for p in range(pairs_per_kv):
                    head = g * group + 2 * p + parity
                    sc = s[p * w:(p + 1) * w] + bias_ref[bias_idx, head]
                    sink = sink_ref[head]
                    m = jnp.maximum(jnp.max(sc, axis=-1, keepdims=True), sink)
                    e = jnp.exp(sc - m)
                    denom = jnp.sum(e, axis=-1, keepdims=True) + jnp.exp(sink - m)
                    p_rows.append((e / denom).astype(BF16))
                probs.append(jnp.concatenate(p_rows, axis=0))
            outs.append(jnp.dot(probs[0], v_lo, preferred_element_type=F32)
                        + jnp.dot(probs[1], v_hi, preferred_element_type=F32))

        o_s = jnp.concatenate(
            [outs[g][p * w:(p + 1) * w] for g in range(SWA_KV_HEADS) for p in range(pairs_per_kv)], axis=1)
        z = ew_ref[r0:r1, EW_Z_S:EW_Z_S + SWA_Q_W]
        tiles.append((o_s * (z * _sigmoid(z))).astype(BF16))

    kprev_ref[...] = ew_ref[ts - w:ts, EW_K_S:EW_K_S + SWA_KV_W]
    vprev_ref[...] = ew_ref[ts - w:ts, EW_V_S:EW_V_S + SWA_KV_W]
    return jnp.concatenate(tiles, axis=0)


def _output_tile(x_ref, a_gla, a_swa, ew_ref, wog_ref, wos_ref, wout_ref, bgate_ref, lng_ref, lnb_ref, o_ref):
    y_gla = jnp.dot(a_gla, wog_ref[...], preferred_element_type=F32)
    y_swa = jnp.dot(a_swa, wos_ref[...], preferred_element_type=F32)
    g_a = _sigmoid(ew_ref[:, EW_GATE:EW_GATE + D_MODEL] + bgate_ref[:, :D_MODEL])
    g_b = _sigmoid(ew_ref[:, EW_GATE + D_MODEL:EW_GATE + GATE_W] + bgate_ref[:, D_MODEL:])
    merged = g_a * y_gla + g_b * y_swa
    y = jnp.dot(merged.astype(BF16), wout_ref[...], preferred_element_type=F32)
    r = DEEPNORM_ALPHA * x_ref[...] + y
    mu = jnp.mean(r, axis=-1, keepdims=True)
    cen = r - mu
    var = jnp.mean(cen * cen, axis=-1, keepdims=True)
    o_ref[...] = cen * lax.rsqrt(var + LN_EPS) * lng_ref[...] + lnb_ref[...]


def _layer_kernel(x_ref, w_ref, wgk_ref, bgk_ref, ng_ref, bias_ref, sink_ref, wog_ref, wos_ref, wout_ref,
                  bgate_ref, lng_ref, lnb_ref, o_ref, mx_ref, ew_ref, state_ref, kprev_ref, vprev_ref):
    first_tile = pl.program_id(1) == 0

    @pl.when(first_tile)
    def _():
        state_ref[...] = jnp.zeros_like(state_ref)
        kprev_ref[...] = jnp.zeros_like(kprev_ref)
        vprev_ref[...] = jnp.zeros_like(vprev_ref)

    _project_tile(x_ref, w_ref, mx_ref, ew_ref)
    a_gla = _gla_tile(mx_ref, ew_ref, wgk_ref, bgk_ref, ng_ref, state_ref)
    a_swa = _swa_tile(mx_ref, ew_ref, bias_ref, sink_ref, kprev_ref, vprev_ref, first_tile)
    _output_tile(x_ref, a_gla, a_swa, ew_ref, wog_ref, wos_ref, wout_ref, bgate_ref, lng_ref, lnb_ref, o_ref)


def _pack_w_in(w_in):
    edges = [0]
    for n in (GLA_QK_W, GLA_QK_W, GLA_V_W, GLA_RANK, GLA_V_W, SWA_Q_W, SWA_KV_W, SWA_KV_W, SWA_Q_W, GATE_W):
        edges.append(edges[-1] + n)
    q_g, k_g, v_g, lr, z_g, q_s, k_s, v_s, z_s, gates = (
        w_in[:, edges[i]:edges[i + 1]] for i in range(10))
    lr_pad = jnp.pad(lr, ((0, 0), (0, RANK_PAD - GLA_RANK)))
    return jnp.concatenate([v_g, lr_pad, q_s, q_g, k_g, z_g, k_s, v_s, z_s, gates], axis=1).astype(BF16)


def kernel(x, w_in, w_gk2, b_gk, gla_norm_g, w_o_gla, sinks, w_o_swa, b_gate, w_out, ln_g, ln_b):
    batch, seq, d_model = x.shape
    assert d_model == D_MODEL and DEPTH == 1
    assert seq % SEQ_TILE == 0 and SEQ_TILE % GLA_CHUNK == 0 and SEQ_TILE % SWA_BLOCK == 0
    steps = seq // SEQ_TILE
    x2d = x.reshape(batch * seq, d_model)

    w_packed = _pack_w_in(w_in)
    w_gk2p = jnp.pad(w_gk2, ((0, RANK_PAD - GLA_RANK), (0, 0))).astype(BF16)
    row = lambda v: v.reshape(1, -1).astype(F32)

    tile = lambda width: pl.BlockSpec((SEQ_TILE, width), lambda b, s: (b * steps + s, 0))
    whole = lambda shape: pl.BlockSpec(shape, lambda b, s: (0,) * len(shape))
    out = pl.pallas_call(
        _layer_kernel,
        grid=(batch, steps),
        in_specs=[
            tile(D_MODEL),
            whole((D_MODEL, PROJ_COLS)),
            whole((RANK_PAD, GLA_QK_W)), whole((1, GLA_QK_W)), whole((1, GLA_DV)),
            whole((2, SWA_HEADS, SWA_BLOCK, 2 * SWA_BLOCK)),
            pl.BlockSpec(memory_space=pltpu.SMEM),
            whole((GLA_V_W, D_MODEL)), whole((SWA_Q_W, D_MODEL)), whole((D_MODEL, D_MODEL)),
            whole((1, GATE_W)), whole((1, D_MODEL)), whole((1, D_MODEL)),
        ],
        out_specs=tile(D_MODEL),
        out_shape=jax.ShapeDtypeStruct((batch * seq, D_MODEL), F32),
        scratch_shapes=[
            pltpu.VMEM((SEQ_TILE, MX_COLS), BF16),
            pltpu.VMEM((SEQ_TILE, EW_COLS), F32),
            pltpu.VMEM((GLA_QK_W, GLA_V_W), F32),
            pltpu.VMEM((SWA_BLOCK, SWA_KV_W), F32),
            pltpu.VMEM((SWA_BLOCK, SWA_KV_W), F32),
        ],
        compiler_params=pltpu.CompilerParams(
            dimension_semantics=("arbitrary", "arbitrary"), vmem_limit_bytes=VMEM_LIMIT_BYTES),
        name="hybrid_layer",
    )(x2d, w_packed, w_gk2p, row(b_gk), row(gla_norm_g), _swa_bias_table(), sinks.astype(F32),
      w_o_gla.astype(BF16), w_o_swa.astype(BF16), w_out.astype(BF16), row(b_gate), row(ln_g), row(ln_b))
    return out.reshape(batch, seq, d_model).astype(x.dtype)
```

```python
import functools

import jax
import jax.numpy as jnp
from jax import lax
from jax.experimental import pallas as pl
from jax.experimental.pallas import tpu as pltpu

F32 = jnp.float32
BF16 = jnp.bfloat16

D_MODEL = 1024
DEPTH = 1
GLA_HEADS = 4
GLA_DK = 64
GLA_DV = 128
GLA_RANK = 16
GLA_CHUNK = 64
GLA_GATE_NORM = 16.0
SWA_HEADS = 8
SWA_KV_HEADS = 2
SWA_HD = 64
SWA_WINDOW = 128
SWA_BLOCK = 128
GLA_QK_W = GLA_HEADS * GLA_DK
GLA_V_W = GLA_HEADS * GLA_DV
SWA_Q_W = SWA_HEADS * SWA_HD
SWA_KV_W = SWA_KV_HEADS * SWA_HD
GATE_W = 2 * D_MODEL
DEEPNORM_ALPHA = (2.0 * DEPTH) ** 0.25
LN_EPS = 1e-5
RMS_EPS = 1e-6

LANES = 128
RANK_PAD = LANES
VMEM_LIMIT_BYTES = 56 * 1024 * 1024

MX_V_G = 0
MX_LR = MX_V_G + GLA_V_W
MX_Q_S = MX_LR + RANK_PAD
MX_COLS = MX_Q_S + SWA_Q_W
EW_Q_G = 0
EW_K_G = EW_Q_G + GLA_QK_W
EW_Z_G = EW_K_G + GLA_QK_W
EW_K_S = EW_Z_G + GLA_V_W
EW_V_S = EW_K_S + SWA_KV_W
EW_Z_S = EW_V_S + SWA_KV_W
EW_GATE = EW_Z_S + SWA_Q_W
EW_COLS = EW_GATE + GATE_W
PROJ_COLS = MX_COLS + EW_COLS

PROJ_COL_CHUNK = 512
SEQ_TILE = 256


def _sigmoid(v):
    return 1.0 / (1.0 + jnp.exp(-v))


def _silu(v):
    return v * _sigmoid(v)


def _log_sigmoid(v):
    return jnp.minimum(v, 0.0) - jnp.log1p(jnp.exp(-jnp.abs(v)))


def _split2(a):
    hi = a.astype(BF16)
    lo = (a - hi.astype(F32)).astype(BF16)
    return hi, lo


def _dot(a, b):
    return jnp.dot(a, b, preferred_element_type=F32)


def _dot_nt(a, b):
    return lax.dot_general(a, b, (((1,), (1,)), ((), ())), preferred_element_type=F32)


def _dot_tn(a, b):
    return lax.dot_general(a, b, (((0,), (0,)), ((), ())), preferred_element_type=F32)


def _projection_steps(x_ref, row0, w_ref, bgate_ref, xb_ref, mx_ref, ew_ref):
    def gate_epilogue(c0):
        return lambda acc: _sigmoid(acc + bgate_ref[:, c0:c0 + PROJ_COL_CHUNK])

    groups = [
        (mx_ref, MX_V_G, GLA_V_W, None),
        (mx_ref, MX_LR, RANK_PAD + SWA_Q_W, None),
        (ew_ref, EW_Q_G, 2 * GLA_QK_W, None),
        (ew_ref, EW_Z_G, GLA_V_W, _silu),
        (ew_ref, EW_K_S, 2 * SWA_KV_W, None),
        (ew_ref, EW_Z_S, SWA_Q_W, _silu),
    ] + [(ew_ref, EW_GATE + c0, PROJ_COL_CHUNK, gate_epilogue(c0)) for c0 in range(0, GATE_W, PROJ_COL_CHUNK)]

    def cast_x():
        xb_ref[...] = x_ref[row0:row0 + SEQ_TILE, :].astype(BF16)

    def make(dst, c0, width, epilogue, first):
        w0 = c0 if dst is mx_ref else MX_COLS + c0

        def step():
            if first:
                cast_x()
            acc = _dot(xb_ref[...], w_ref[:, w0:w0 + width])
            if epilogue is not None:
                acc = epilogue(acc)
            dst[:, c0:c0 + width] = acc.astype(dst.dtype)
        return step

    return [make(*g, first=(i == 0)) for i, g in enumerate(groups)]


class _Interleaver:
    def __init__(self, steps):
        self._steps = list(steps)

    def __call__(self, n=1):
        for _ in range(min(n, len(self._steps))):
            self._steps.pop(0)()

    def flush(self):
        self(len(self._steps))


class _GlaTile:
    def __init__(self, mx_ref, ew_ref, wgk_ref, bgk_ref, ng_ref, state_ref):
        self.mx, self.ew = mx_ref, ew_ref
        self.wgk, self.bgk, self.ng, self.state = wgk_ref, bgk_ref, ng_ref, state_ref
        self.ts = mx_ref.shape[0]
        self.n_chunks = self.ts // GLA_CHUNK

    def gate_logits(self):
        self.gk = _dot(self.mx[:, MX_LR:MX_LR + RANK_PAD], self.wgk[...]) + self.bgk[...]

    def cumsum(self):
        ts = self.ts
        log_g = _log_sigmoid(self.gk) * (1.0 / GLA_GATE_NORM)
        row = lax.broadcasted_iota(jnp.int32, (ts, ts), 0)
        col = lax.broadcasted_iota(jnp.int32, (ts, ts), 1)
        same_chunk = (row // GLA_CHUNK) == (col // GLA_CHUNK)
        lower = jnp.where(same_chunk & (col <= row), 1.0, 0.0).astype(BF16)
        upper = jnp.where(same_chunk & (col > row), 1.0, 0.0).astype(BF16)
        lg_hi, lg_lo = _split2(log_g)
        self.g_cum = _dot(lower, lg_hi) + _dot(lower, lg_lo)
        self.g_rev = _dot(upper, lg_hi) + _dot(upper, lg_lo)

    def chunk_scores(self):
        r_k = lax.broadcasted_iota(jnp.int32, (GLA_QK_W, GLA_QK_W), 0)
        c_k = lax.broadcasted_iota(jnp.int32, (GLA_QK_W, GLA_QK_W), 1)
        k_head_mask = (r_k // GLA_CHUNK) == (c_k // GLA_DK)
        r_v = lax.broadcasted_iota(jnp.int32, (GLA_QK_W, GLA_V_W), 0)
        c_v = lax.broadcasted_iota(jnp.int32, (GLA_QK_W, GLA_V_W), 1)
        self.v_head_mask = (r_v // GLA_DK) == (c_v // GLA_DV)
        g_cum_t = self.g_cum.T
        self.q_dec, self.scores, self.kv, self.decay = [], [], [], []
        for c in range(self.n_chunks):
            r0, r1 = c * GLA_CHUNK, (c + 1) * GLA_CHUNK
            g_c = self.g_cum[r0:r1]
            q_c = self.ew[r0:r1, EW_Q_G:EW_Q_G + GLA_QK_W]
            k_c = self.ew[r0:r1, EW_K_G:EW_K_G + GLA_QK_W]
            q_dec = (q_c * ((GLA_DK ** -0.5) * jnp.exp(g_c))).astype(BF16)
            k_inv = (k_c * jnp.exp(-g_c)).astype(BF16)
            k_end = (k_c * jnp.exp(self.g_rev[r0:r1])).astype(BF16)
            k_bd = jnp.where(k_head_mask, jnp.concatenate([k_inv] * GLA_HEADS, axis=0), jnp.zeros((), BF16))
            self.q_dec.append(q_dec)
            self.scores.append(_dot_nt(q_dec, k_bd))
            self.kv.append(_dot_tn(k_end, self.mx[r0:r1, MX_V_G:MX_V_G + GLA_V_W]))
            self.decay.append(jnp.exp(g_cum_t[:, r1 - 1:r1]))

    def chunk_outputs(self):
        r_a = lax.broadcasted_iota(jnp.int32, (GLA_CHUNK, GLA_QK_W), 0)
        c_a = lax.broadcasted_iota(jnp.int32, (GLA_CHUNK, GLA_QK_W), 1)
        causal = (c_a % GLA_CHUNK) <= r_a
        state = self.state[...]
        o_chunks = []
        for c in range(self.n_chunks):
            r0, r1 = c * GLA_CHUNK, (c + 1) * GLA_CHUNK
            a = jnp.where(causal, self.scores[c], 0.0).astype(BF16)
            v_c = self.mx[r0:r1, MX_V_G:MX_V_G + GLA_V_W]
            v_bd = jnp.where(self.v_head_mask, jnp.concatenate([v_c] * GLA_HEADS, axis=0), jnp.zeros((), BF16))
            o_chunks.append(_dot(a, v_bd) + _dot(self.q_dec[c], state.astype(BF16)))
            state = state * self.decay[c] + jnp.where(self.v_head_mask, self.kv[c], 0.0)
        self.state[...] = state
        self.o = jnp.concatenate(o_chunks, axis=0)

    def finish(self):
        normed = []
        for hd in range(GLA_HEADS):
            o_h = self.o[:, hd * GLA_DV:(hd + 1) * GLA_DV]
            ms = jnp.mean(o_h * o_h, axis=-1, keepdims=True)
            normed.append(o_h * lax.rsqrt(ms + RMS_EPS) * self.ng[...])
        o_n = jnp.concatenate(normed, axis=1)
        return (o_n * self.ew[:, EW_Z_G:EW_Z_G + GLA_V_W]).astype(BF16)


def _swa_bias_table():
    w = SWA_BLOCK
    slopes = 2.0 ** (-8.0 * jnp.arange(1, SWA_HEADS + 1, dtype=F32) / SWA_HEADS)
    q_idx = jnp.arange(w)[:, None]
    k_idx = jnp.arange(2 * w)[None, :]
    dist = q_idx - (k_idx - w)
    valid = (dist >= 0) & (dist < SWA_WINDOW)
    bias = -slopes[:, None, None] * dist.astype(F32)[None]
    rest = jnp.where(valid[None], bias, -jnp.inf)
    first = jnp.where((valid & (k_idx >= w))[None], bias, -jnp.inf)
    return jnp.stack([first, rest]).astype(F32)


class _SwaTile:
    GROUP = SWA_HEADS // SWA_KV_HEADS
    PAIRS = GROUP // 2

    def __init__(self, mx_ref, ew_ref, bias_ref, sink_ref, kprev_ref, vprev_ref, maybe_first):
        self.mx, self.ew = mx_ref, ew_ref
        self.bias, self.sink, self.kprev, self.vprev = bias_ref, sink_ref, kprev_ref, vprev_ref
        self.maybe_first = maybe_first
        self.ts = mx_ref.shape[0]
        self.n_blocks = self.ts // SWA_BLOCK
        self.s, self.v_sides, self.outs = {}, {}, {}

    def scores(self, n):
        w = SWA_BLOCK
        r0, r1 = n * w, (n + 1) * w
        if n == 0:
            k_prev, v_prev = self.kprev[...], self.vprev[...]
        else:
            k_prev = self.ew[r0 - w:r0, EW_K_S:EW_K_S + SWA_KV_W]
            v_prev = self.ew[r0 - w:r0, EW_V_S:EW_V_S + SWA_KV_W]
        k_cat = jnp.concatenate([k_prev, self.ew[r0:r1, EW_K_S:EW_K_S + SWA_KV_W]], axis=0)
        v_cat = jnp.concatenate([v_prev, self.ew[r0:r1, EW_V_S:EW_V_S + SWA_KV_W]], axis=0)
        k_swap = pltpu.roll(k_cat, SWA_HD, 1)
        v_swap = pltpu.roll(v_cat, SWA_HD, 1)
        low = lax.broadcasted_iota(jnp.int32, (2 * w, SWA_KV_W), 1) < SWA_HD
        for g in range(SWA_KV_HEADS):
            k_low, k_high = (k_cat, k_swap) if g == 0 else (k_swap, k_cat)
            v_low, v_high = (v_cat, v_swap) if g == 0 else (v_swap, v_cat)
            k_sides = (jnp.where(low, k_low, 0.0).astype(BF16), jnp.where(low, 0.0, k_high).astype(BF16))
            self.v_sides[n, g] = (jnp.where(low, v_low, 0.0).astype(BF16),
                                  jnp.where(low, 0.0, v_high).astype(BF16))
            q_st = jnp.concatenate(
                [self.mx[r0:r1, MX_Q_S + (g * self.PAIRS + p) * LANES:MX_Q_S + (g * self.PAIRS + p + 1) * LANES]
                 for p in range(self.PAIRS)], axis=0)
            for parity in range(2):
                self.s[n, g, parity] = _dot_nt(q_st, k_sides[parity])

    def attend(self, n, g):
        w = SWA_BLOCK
        if n == 0 and self.maybe_first is not None:
            bias_idx = jnp.where(self.maybe_first, 0, 1)
        else:
            bias_idx = 1
        probs = []
        for parity in range(2):
            s = self.s[n, g, parity] * (SWA_HD ** -0.5)
            p_rows = []
            for p in range(self.PAIRS):
                head = g * self.GROUP + 2 * p + parity
                sc = s[p * w:(p + 1) * w] + self.bias[bias_idx, head]
                sink = self.sink[head]
                m = jnp.maximum(jnp.max(sc, axis=-1, keepdims=True), sink)
                e = jnp.exp(sc - m)
                denom = jnp.sum(e, axis=-1, keepdims=True) + jnp.exp(sink - m)
                p_rows.append((e / denom).astype(BF16))
            probs.append(jnp.concatenate(p_rows, axis=0))
        v_lo, v_hi = self.v_sides[n, g]
        self.outs[n, g] = _dot(probs[0], v_lo) + _dot(probs[1], v_hi)

    def finish(self):
        w = SWA_BLOCK
        tiles = []
        for n in range(self.n_blocks):
            o_s = jnp.concatenate(
                [self.outs[n, g][p * w:(p + 1) * w] for g in range(SWA_KV_HEADS) for p in range(self.PAIRS)],
                axis=1)
            tiles.append((o_s * self.ew[n * w:(n + 1) * w, EW_Z_S:EW_Z_S + SWA_Q_W]).astype(BF16))
        self.kprev[...] = self.ew[self.ts - w:self.ts, EW_K_S:EW_K_S + SWA_KV_W]
        self.vprev[...] = self.ew[self.ts - w:self.ts, EW_V_S:EW_V_S + SWA_KV_W]
        return jnp.concatenate(tiles, axis=0)


def _layer_kernel(tiles_per_seq, x_ref, xnext_ref, w_ref, wgk_ref, bgk_ref, ng_ref, bias_ref, sink_ref,
                  wog_ref, wos_ref, wout_ref, bgate_ref, lng_ref, lnb_ref, o_ref,
                  xb_ref, mx0_ref, ew0_ref, mx1_ref, ew1_ref, state_ref, kprev_ref, vprev_ref):
    j = pl.program_id(0)
    first_tile = (2 * j) % tiles_per_seq == 0

    def projection(src_ref, row0, mx_ref, ew_ref):
        return _projection_steps(src_ref, row0, w_ref, bgate_ref, xb_ref, mx_ref, ew_ref)

    @pl.when(j == 0)
    def _():
        _Interleaver(projection(x_ref, 0, mx0_ref, ew0_ref)).flush()

    @pl.when(first_tile)
    def _():
        state_ref[...] = jnp.zeros_like(state_ref)
        kprev_ref[...] = jnp.zeros_like(kprev_ref)
        vprev_ref[...] = jnp.zeros_like(vprev_ref)

    def stages(row0, mx_ref, ew_ref, maybe_first, next_projection):
        tick = _Interleaver(next_projection)
        gla = _GlaTile(mx_ref, ew_ref, wgk_ref, bgk_ref, ng_ref, state_ref)
        swa = _SwaTile(mx_ref, ew_ref, bias_ref, sink_ref, kprev_ref, vprev_ref, maybe_first)
        swa.scores(0)
        gla.gate_logits()
        tick()
        swa.scores(1)
        gla.cumsum()
        tick()
        swa.attend(0, 0)
        tick()
        gla.chunk_scores()
        swa.attend(0, 1)
        tick()
        gla.chunk_outputs()
        swa.attend(1, 0)
        tick()
        swa.attend(1, 1)
        tick()
        a_gla = gla.finish()
        a_swa = swa.finish()
        y_gla = _dot(a_gla, wog_ref[...])
        y_swa = _dot(a_swa, wos_ref[...])
        tick()
        merged = (ew_ref[:, EW_GATE:EW_GATE + D_MODEL] * y_gla
                  + ew_ref[:, EW_GATE + D_MODEL:EW_GATE + GATE_W] * y_swa)
        tick()
        y = _dot(merged.astype(BF16), wout_ref[...])
        tick()
        r = DEEPNORM_ALPHA * x_ref[row0:row0 + SEQ_TILE, :] + y
        mu = jnp.mean(r, axis=-1, keepdims=True)
        cen = r - mu
        var = jnp.mean(cen * cen, axis=-1, keepdims=True)
        o_ref[row0:row0 + SEQ_TILE, :] = cen * lax.rsqrt(var + LN_EPS) * lng_ref[...] + lnb_ref[...]
        tick.flush()

    stages(0, mx0_ref, ew0_ref, first_tile, projection(x_ref, SEQ_TILE, mx1_ref, ew1_ref))
    stages(SEQ_TILE, mx1_ref, ew1_ref, None, projection(xnext_ref, 0, mx0_ref, ew0_ref))


def _pack_w_in(w_in):
    edges = [0]
    for n in (GLA_QK_W, GLA_QK_W, GLA_V_W, GLA_RANK, GLA_V_W, SWA_Q_W, SWA_KV_W, SWA_KV_W, SWA_Q_W, GATE_W):
        edges.append(edges[-1] + n)
    q_g, k_g, v_g, lr, z_g, q_s, k_s, v_s, z_s, gates = (
        w_in[:, edges[i]:edges[i + 1]] for i in range(10))
    lr_pad = jnp.pad(lr, ((0, 0), (0, RANK_PAD - GLA_RANK)))
    return jnp.concatenate([v_g, lr_pad, q_s, q_g, k_g, z_g, k_s, v_s, z_s, gates], axis=1).astype(BF16)


def kernel(x, w_in, w_gk2, b_gk, gla_norm_g, w_o_gla, sinks, w_o_swa, b_gate, w_out, ln_g, ln_b):
    batch, seq, d_model = x.shape
    assert d_model == D_MODEL and DEPTH == 1
    assert seq % (2 * SEQ_TILE) == 0 and SEQ_TILE % GLA_CHUNK == 0 and SEQ_TILE % SWA_BLOCK == 0
    tiles_per_seq = seq // SEQ_TILE
    n_tiles = batch * tiles_per_seq
    x2d = x.reshape(batch * seq, d_model)

    w_packed = _pack_w_in(w_in)
    w_gk2p = jnp.pad(w_gk2, ((0, RANK_PAD - GLA_RANK), (0, 0))).astype(BF16)
    row = lambda v: v.reshape(1, -1).astype(F32)

    pair = pl.BlockSpec((2 * SEQ_TILE, D_MODEL), lambda j: (j, 0))
    nxt = pl.BlockSpec((SEQ_TILE, D_MODEL), lambda j: (jnp.minimum(2 * j + 2, n_tiles - 1), 0))
    whole = lambda shape: pl.BlockSpec(shape, lambda j: (0,) * len(shape), pipeline_mode=pl.Buffered(1))
    out = pl.pallas_call(
        functools.partial(_layer_kernel, tiles_per_seq),
        grid=(n_tiles // 2,),
        in_specs=[
            pair, nxt,
            whole((D_MODEL, PROJ_COLS)),
            whole((RANK_PAD, GLA_QK_W)), whole((1, GLA_QK_W)), whole((1, GLA_DV)),
            whole((2, SWA_HEADS, SWA_BLOCK, 2 * SWA_BLOCK)),
            pl.BlockSpec(memory_space=pltpu.SMEM),
            whole((GLA_V_W, D_MODEL)), whole((SWA_Q_W, D_MODEL)), whole((D_MODEL, D_MODEL)),
            whole((1, GATE_W)), whole((1, D_MODEL)), whole((1, D_MODEL)),
        ],
        out_specs=pair,
        out_shape=jax.ShapeDtypeStruct((batch * seq, D_MODEL), F32),
        scratch_shapes=[
            pltpu.VMEM((SEQ_TILE, D_MODEL), BF16),
            pltpu.VMEM((SEQ_TILE, MX_COLS), BF16),
            pltpu.VMEM((SEQ_TILE, EW_COLS), F32),
            pltpu.VMEM((SEQ_TILE, MX_COLS), BF16),
            pltpu.VMEM((SEQ_TILE, EW_COLS), F32),
            pltpu.VMEM((GLA_QK_W, GLA_V_W), F32),
            pltpu.VMEM((SWA_BLOCK, SWA_KV_W), F32),
            pltpu.VMEM((SWA_BLOCK, SWA_KV_W), F32),
        ],
        compiler_params=pltpu.CompilerParams(
            dimension_semantics=("arbitrary",), vmem_limit_bytes=VMEM_LIMIT_BYTES),
        name="hybrid_layer",
    )(x2d, x2d, w_packed, w_gk2p, row(b_gk), row(gla_norm_g), _swa_bias_table(), sinks.astype(F32),
      w_o_gla.astype(BF16), w_o_swa.astype(BF16), w_out.astype(BF16), row(b_gate), row(ln_g), row(ln_b))
    return out.reshape(batch, seq, d_model).astype(x.dtype)
```

```python
import functools

import jax
import jax.numpy as jnp
from jax import lax
from jax.experimental import pallas as pl
from jax.experimental.pallas import tpu as pltpu

F32 = jnp.float32
BF16 = jnp.bfloat16

D_MODEL = 1024
DEPTH = 1
GLA_HEADS = 4
GLA_DK = 64
GLA_DV = 128
GLA_RANK = 16
GLA_CHUNK = 64
GLA_GATE_NORM = 16.0
SWA_HEADS = 8
SWA_KV_HEADS = 2
SWA_HD = 64
SWA_WINDOW = 128
SWA_BLOCK = 128
GLA_QK_W = GLA_HEADS * GLA_DK
GLA_V_W = GLA_HEADS * GLA_DV
SWA_Q_W = SWA_HEADS * SWA_HD
SWA_KV_W = SWA_KV_HEADS * SWA_HD
GATE_W = 2 * D_MODEL
DEEPNORM_ALPHA = (2.0 * DEPTH) ** 0.25
LN_EPS = 1e-5
RMS_EPS = 1e-6

LANES = 128
RANK_PAD = LANES
VMEM_LIMIT_BYTES = 56 * 1024 * 1024

MX_V_G = 0
MX_LR = MX_V_G + GLA_V_W
MX_Q_S = MX_LR + RANK_PAD
MX_COLS = MX_Q_S + SWA_Q_W
EW_Q_G = 0
EW_K_G = EW_Q_G + GLA_QK_W
EW_Z_G = EW_K_G + GLA_QK_W
EW_K_S = EW_Z_G + GLA_V_W
EW_V_S = EW_K_S + SWA_KV_W
EW_Z_S = EW_V_S + SWA_KV_W
EW_GATE = EW_Z_S + SWA_Q_W
EW_COLS = EW_GATE + GATE_W
PROJ_COLS = MX_COLS + EW_COLS

PROJ_COL_CHUNK = 512
SEQ_TILE = 256


def _sigmoid(v):
    return 1.0 / (1.0 + jnp.exp(-v))


def _silu(v):
    return v * _sigmoid(v)


def _log_sigmoid(v):
    return jnp.minimum(v, 0.0) - jnp.log1p(jnp.exp(-jnp.abs(v)))


def _split2(a):
    hi = a.astype(BF16)
    lo = (a - hi.astype(F32)).astype(BF16)
    return hi, lo


def _dot(a, b):
    return jnp.dot(a, b, preferred_element_type=F32)


def _dot_nt(a, b):
    return lax.dot_general(a, b, (((1,), (1,)), ((), ())), preferred_element_type=F32)


def _dot_tn(a, b):
    return lax.dot_general(a, b, (((0,), (0,)), ((), ())), preferred_element_type=F32)


def _projection_steps(x_ref, row0, w_ref, bgate_ref, xb_ref, mx_ref, ew_ref):
    def gate_epilogue(c0):
        return lambda acc: _sigmoid(acc + bgate_ref[:, c0:c0 + PROJ_COL_CHUNK])

    groups = [
        (mx_ref, MX_V_G, GLA_V_W, None),
        (mx_ref, MX_LR, RANK_PAD + SWA_Q_W, None),
        (ew_ref, EW_Q_G, 2 * GLA_QK_W, None),
        (ew_ref, EW_Z_G, GLA_V_W, _silu),
        (ew_ref, EW_K_S, 2 * SWA_KV_W, None),
        (ew_ref, EW_Z_S, SWA_Q_W, _silu),
    ] + [(ew_ref, EW_GATE + c0, PROJ_COL_CHUNK, gate_epilogue(c0)) for c0 in range(0, GATE_W, PROJ_COL_CHUNK)]

    def cast_x():
        xb_ref[...] = x_ref[row0:row0 + SEQ_TILE, :].astype(BF16)

    def make(dst, c0, width, epilogue, first):
        w0 = c0 if dst is mx_ref else MX_COLS + c0

        def step():
            if first:
                cast_x()
            acc = _dot(xb_ref[...], w_ref[:, w0:w0 + width])
            if epilogue is not None:
                acc = epilogue(acc)
            dst[:, c0:c0 + width] = acc.astype(dst.dtype)
        return step

    return [make(*g, first=(i == 0)) for i, g in enumerate(groups)]


class _Interleaver:
    def __init__(self, steps):
        self._steps = list(steps)

    def __call__(self, n=1):
        for _ in range(min(n, len(self._steps))):
            self._steps.pop(0)()

    def flush(self):
        self(len(self._steps))


class _GlaTile:
    PAIRS = GLA_HEADS // 2
    PAIR_K = 2 * GLA_DK
    PAIR_V = 2 * GLA_DV

    def __init__(self, mx_ref, ew_ref, wgk_ref, bgk_ref, ng_ref, state_ref):
        self.mx, self.ew = mx_ref, ew_ref
        self.wgk, self.bgk, self.ng, self.state = wgk_ref, bgk_ref, ng_ref, state_ref
        self.ts = mx_ref.shape[0]
        self.n_chunks = self.ts // GLA_CHUNK

    def gate_logits(self):
        self.gk = _dot(self.mx[:, MX_LR:MX_LR + RANK_PAD], self.wgk[...]) + self.bgk[...]

    def cumsum(self):
        ts = self.ts
        log_g = _log_sigmoid(self.gk) * (1.0 / GLA_GATE_NORM)
        row = lax.broadcasted_iota(jnp.int32, (ts, ts), 0)
        col = lax.broadcasted_iota(jnp.int32, (ts, ts), 1)
        lower = jnp.where(((row // GLA_CHUNK) == (col // GLA_CHUNK)) & (col <= row), 1.0, 0.0).astype(BF16)
        lg_hi, lg_lo = _split2(log_g)
        self.g_cum = _dot(lower, lg_hi) + _dot(lower, lg_lo)

    def chunk_scores(self):
        r_k = lax.broadcasted_iota(jnp.int32, (GLA_QK_W, GLA_QK_W), 0)
        c_k = lax.broadcasted_iota(jnp.int32, (GLA_QK_W, GLA_QK_W), 1)
        k_head_mask = (r_k // GLA_CHUNK) == (c_k // GLA_DK)
        r_v = lax.broadcasted_iota(jnp.int32, (self.PAIR_K, self.PAIR_V), 0)
        c_v = lax.broadcasted_iota(jnp.int32, (self.PAIR_K, self.PAIR_V), 1)
        self.pair_mask = (r_v // GLA_DK) == (c_v // GLA_DV)
        g_cum_t = self.g_cum.T
        self.q_dec, self.scores, self.kv, self.decay = [], [], [], []
        for c in range(self.n_chunks):
            r0, r1 = c * GLA_CHUNK, (c + 1) * GLA_CHUNK
            g_c = self.g_cum[r0:r1]
            g_rev = self.g_cum[r1 - 1:r1] - g_c
            q_c = self.ew[r0:r1, EW_Q_G:EW_Q_G + GLA_QK_W]
            k_c = self.ew[r0:r1, EW_K_G:EW_K_G + GLA_QK_W]
            q_dec = (q_c * ((GLA_DK ** -0.5) * jnp.exp(g_c))).astype(BF16)
            k_inv = (k_c * jnp.exp(-g_c)).astype(BF16)
            k_end = (k_c * jnp.exp(g_rev)).astype(BF16)
            k_bd = jnp.where(k_head_mask, jnp.concatenate([k_inv] * GLA_HEADS, axis=0), jnp.zeros((), BF16))
            self.q_dec.append(q_dec)
            self.scores.append(_dot_nt(q_dec, k_bd))
            self.kv.append([
                _dot_tn(k_end[:, p * self.PAIR_K:(p + 1) * self.PAIR_K],
                        self.mx[r0:r1, MX_V_G + p * self.PAIR_V:MX_V_G + (p + 1) * self.PAIR_V])
                for p in range(self.PAIRS)])
            self.decay.append([jnp.exp(g_cum_t[p * self.PAIR_K:(p + 1) * self.PAIR_K, r1 - 1:r1])
                               for p in range(self.PAIRS)])

    def chunk_outputs(self):
        r_a = lax.broadcasted_iota(jnp.int32, (GLA_CHUNK, GLA_QK_W), 0)
        c_a = lax.broadcasted_iota(jnp.int32, (GLA_CHUNK, GLA_QK_W), 1)
        causal = (c_a % GLA_CHUNK) <= r_a
        states = [self.state[p] for p in range(self.PAIRS)]
        o_chunks = []
        for c in range(self.n_chunks):
            r0, r1 = c * GLA_CHUNK, (c + 1) * GLA_CHUNK
            a = jnp.where(causal, self.scores[c], 0.0).astype(BF16)
            o_pairs = []
            for p in range(self.PAIRS):
                k0, k1 = p * self.PAIR_K, (p + 1) * self.PAIR_K
                v_p = self.mx[r0:r1, MX_V_G + p * self.PAIR_V:MX_V_G + (p + 1) * self.PAIR_V]
                v_bd = jnp.where(self.pair_mask, jnp.concatenate([v_p, v_p], axis=0), jnp.zeros((), BF16))
                o_pairs.append(_dot(a[:, k0:k1], v_bd) + _dot(self.q_dec[c][:, k0:k1], states[p].astype(BF16)))
                states[p] = states[p] * self.decay[c][p] + jnp.where(self.pair_mask, self.kv[c][p], 0.0)
            o_chunks.append(jnp.concatenate(o_pairs, axis=1))
        for p in range(self.PAIRS):
            self.state[p] = states[p]
        self.o = jnp.concatenate(o_chunks, axis=0)

    def finish(self):
        normed = []
        for hd in range(GLA_HEADS):
            o_h = self.o[:, hd * GLA_DV:(hd + 1) * GLA_DV]
            ms = jnp.mean(o_h * o_h, axis=-1, keepdims=True)
            normed.append(o_h * lax.rsqrt(ms + RMS_EPS) * self.ng[...])
        o_n = jnp.concatenate(normed, axis=1)
        return (o_n * self.ew[:, EW_Z_G:EW_Z_G + GLA_V_W]).astype(BF16)


def _swa_bias_table():
    w = SWA_BLOCK
    slopes = 2.0 ** (-8.0 * jnp.arange(1, SWA_HEADS + 1, dtype=F32) / SWA_HEADS)
    q_idx = jnp.arange(w)[:, None]
    k_idx = jnp.arange(2 * w)[None, :]
    dist = q_idx - (k_idx - w)
    valid = (dist >= 0) & (dist < SWA_WINDOW)
    bias = -slopes[:, None, None] * dist.astype(F32)[None]
    rest = jnp.where(valid[None], bias, -jnp.inf)
    first = jnp.where((valid & (k_idx >= w))[None], bias, -jnp.inf)
    return jnp.stack([first, rest]).astype(F32)


class _SwaTile:
    GROUP = SWA_HEADS // SWA_KV_HEADS
    PAIRS = GROUP // 2

    def __init__(self, mx_ref, ew_ref, bias_ref, sink_ref, kprev_ref, vprev_ref, maybe_first):
        self.mx, self.ew = mx_ref, ew_ref
        self.bias, self.sink, self.kprev, self.vprev = bias_ref, sink_ref, kprev_ref, vprev_ref
        self.maybe_first = maybe_first
        self.ts = mx_ref.shape[0]
        self.n_blocks = self.ts // SWA_BLOCK
        self.s, self.v_sides, self.outs = {}, {}, {}

    def scores(self, n):
        w = SWA_BLOCK
        r0, r1 = n * w, (n + 1) * w
        if n == 0:
            k_prev, v_prev = self.kprev[...], self.vprev[...]
        else:
            k_prev = self.ew[r0 - w:r0, EW_K_S:EW_K_S + SWA_KV_W]
            v_prev = self.ew[r0 - w:r0, EW_V_S:EW_V_S + SWA_KV_W]
        k_cat = jnp.concatenate([k_prev, self.ew[r0:r1, EW_K_S:EW_K_S + SWA_KV_W]], axis=0)
        v_cat = jnp.concatenate([v_prev, self.ew[r0:r1, EW_V_S:EW_V_S + SWA_KV_W]], axis=0)
        k_swap = pltpu.roll(k_cat, SWA_HD, 1)
        v_swap = pltpu.roll(v_cat, SWA_HD, 1)
        low = lax.broadcasted_iota(jnp.int32, (2 * w, SWA_KV_W), 1) < SWA_HD
        for g in range(SWA_KV_HEADS):
            k_low, k_high = (k_cat, k_swap) if g == 0 else (k_swap, k_cat)
            v_low, v_high = (v_cat, v_swap) if g == 0 else (v_swap, v_cat)
            k_sides = (jnp.where(low, k_low, 0.0).astype(BF16), jnp.where(low, 0.0, k_high).astype(BF16))
            self.v_sides[n, g] = (jnp.where(low, v_low, 0.0).astype(BF16),
                                  jnp.where(low, 0.0, v_high).astype(BF16))
            q_st = jnp.concatenate(
                [self.mx[r0:r1, MX_Q_S + (g * self.PAIRS + p) * LANES:MX_Q_S + (g * self.PAIRS + p + 1) * LANES]
                 for p in range(self.PAIRS)], axis=0)
            for parity in range(2):
                self.s[n, g, parity] = _dot_nt(q_st, k_sides[parity])

    def attend(self, n, g):
        w = SWA_BLOCK
        if n == 0 and self.maybe_first is not None:
            bias_idx = jnp.where(self.maybe_first, 0, 1)
        else:
            bias_idx = 1
        probs = []
        for parity in range(2):
            s = self.s[n, g, parity] * (SWA_HD ** -0.5)
            p_rows = []
            for p in range(self.PAIRS):
                head = g * self.GROUP + 2 * p + parity
                sc = s[p * w:(p + 1) * w] + self.bias[bias_idx, head]
                sink = self.sink[head]
                m = jnp.maximum(jnp.max(sc, axis=-1, keepdims=True), sink)
                e = jnp.exp(sc - m)
                denom = jnp.sum(e, axis=-1, keepdims=True) + jnp.exp(sink - m)
                p_rows.append((e / denom).astype(BF16))
            probs.append(jnp.concatenate(p_rows, axis=0))
        v_lo, v_hi = self.v_sides[n, g]
        self.outs[n, g] = _dot(probs[0], v_lo) + _dot(probs[1], v_hi)

    def finish(self):
        w = SWA_BLOCK
        tiles = []
        for n in range(self.n_blocks):
            o_s = jnp.concatenate(
                [self.outs[n, g][p * w:(p + 1) * w] for g in range(SWA_KV_HEADS) for p in range(self.PAIRS)],
                axis=1)
            tiles.append((o_s * self.ew[n * w:(n + 1) * w, EW_Z_S:EW_Z_S + SWA_Q_W]).astype(BF16))
        self.kprev[...] = self.ew[self.ts - w:self.ts, EW_K_S:EW_K_S + SWA_KV_W]
        self.vprev[...] = self.ew[self.ts - w:self.ts, EW_V_S:EW_V_S + SWA_KV_W]
        return jnp.concatenate(tiles, axis=0)


def _layer_kernel(tiles_per_seq, x_ref, xnext_ref, w_ref, wgk_ref, bgk_ref, ng_ref, bias_ref, sink_ref,
                  wog_ref, wos_ref, wout_ref, bgate_ref, lng_ref, lnb_ref, o_ref,
                  xb_ref, mx0_ref, ew0_ref, mx1_ref, ew1_ref, state_ref, kprev_ref, vprev_ref):
    j = pl.program_id(0)
    first_tile = (2 * j) % tiles_per_seq == 0

    def projection(src_ref, row0, mx_ref, ew_ref):
        return _projection_steps(src_ref, row0, w_ref, bgate_ref, xb_ref, mx_ref, ew_ref)

    @pl.when(j == 0)
    def _():
        _Interleaver(projection(x_ref, 0, mx0_ref, ew0_ref)).flush()

    @pl.when(first_tile)
    def _():
        state_ref[...] = jnp.zeros_like(state_ref)
        kprev_ref[...] = jnp.zeros_like(kprev_ref)
        vprev_ref[...] = jnp.zeros_like(vprev_ref)

    def stages(row0, mx_ref, ew_ref, maybe_first, next_projection):
        tick = _Interleaver(next_projection)
        gla = _GlaTile(mx_ref, ew_ref, wgk_ref, bgk_ref, ng_ref, state_ref)
        swa = _SwaTile(mx_ref, ew_ref, bias_ref, sink_ref, kprev_ref, vprev_ref, maybe_first)
        swa.scores(0)
        gla.gate_logits()
        tick()
        swa.scores(1)
        gla.cumsum()
        tick()
        swa.attend(0, 0)
        tick()
        gla.chunk_scores()
        swa.attend(0, 1)
        tick()
        gla.chunk_outputs()
        swa.attend(1, 0)
        tick()
        swa.attend(1, 1)
        tick()
        a_gla = gla.finish()
        a_swa = swa.finish()
        y_gla = _dot(a_gla, wog_ref[...])
        y_swa = _dot(a_swa, wos_ref[...])
        tick()
        merged = (ew_ref[:, EW_GATE:EW_GATE + D_MODEL] * y_gla
                  + ew_ref[:, EW_GATE + D_MODEL:EW_GATE + GATE_W] * y_swa)
        tick()
        y = _dot(merged.astype(BF16), wout_ref[...])
        tick()
        r = DEEPNORM_ALPHA * x_ref[row0:row0 + SEQ_TILE, :] + y
        mu = jnp.mean(r, axis=-1, keepdims=True)
        cen = r - mu
        var = jnp.mean(cen * cen, axis=-1, keepdims=True)
        o_ref[row0:row0 + SEQ_TILE, :] = cen * lax.rsqrt(var + LN_EPS) * lng_ref[...] + lnb_ref[...]
        tick.flush()

    stages(0, mx0_ref, ew0_ref, first_tile, projection(x_ref, SEQ_TILE, mx1_ref, ew1_ref))
    stages(SEQ_TILE, mx1_ref, ew1_ref, None, projection(xnext_ref, 0, mx0_ref, ew0_ref))


def _pack_w_in(w_in):
    edges = [0]
    for n in (GLA_QK_W, GLA_QK_W, GLA_V_W, GLA_RANK, GLA_V_W, SWA_Q_W, SWA_KV_W, SWA_KV_W, SWA_Q_W, GATE_W):
        edges.append(edges[-1] + n)
    q_g, k_g, v_g, lr, z_g, q_s, k_s, v_s, z_s, gates = (
        w_in[:, edges[i]:edges[i + 1]] for i in range(10))
    lr_pad = jnp.pad(lr, ((0, 0), (0, RANK_PAD - GLA_RANK)))
    return jnp.concatenate([v_g, lr_pad, q_s, q_g, k_g, z_g, k_s, v_s, z_s, gates], axis=1).astype(BF16)


def kernel(x, w_in, w_gk2, b_gk, gla_norm_g, w_o_gla, sinks, w_o_swa, b_gate, w_out, ln_g, ln_b):
    batch, seq, d_model = x.shape
    assert d_model == D_MODEL and DEPTH == 1
    assert seq % (2 * SEQ_TILE) == 0 and SEQ_TILE % GLA_CHUNK == 0 and SEQ_TILE % SWA_BLOCK == 0
    tiles_per_seq = seq // SEQ_TILE
    n_tiles = batch * tiles_per_seq
    x2d = x.reshape(batch * seq, d_model)

    w_packed = _pack_w_in(w_in)
    w_gk2p = jnp.pad(w_gk2, ((0, RANK_PAD - GLA_RANK), (0, 0))).astype(BF16)
    row = lambda v: v.reshape(1, -1).astype(F32)

    pair = pl.BlockSpec((2 * SEQ_TILE, D_MODEL), lambda j: (j, 0))
    nxt = pl.BlockSpec((SEQ_TILE, D_MODEL), lambda j: (jnp.minimum(2 * j + 2, n_tiles - 1), 0))
    whole = lambda shape: pl.BlockSpec(shape, lambda j: (0,) * len(shape), pipeline_mode=pl.Buffered(1))
    out = pl.pallas_call(
        functools.partial(_layer_kernel, tiles_per_seq),
        grid=(n_tiles // 2,),
        in_specs=[
            pair, nxt,
            whole((D_MODEL, PROJ_COLS)),
            whole((RANK_PAD, GLA_QK_W)), whole((1, GLA_QK_W)), whole((1, GLA_DV)),
            whole((2, SWA_HEADS, SWA_BLOCK, 2 * SWA_BLOCK)),
            pl.BlockSpec(memory_space=pltpu.SMEM),
            whole((GLA_V_W, D_MODEL)), whole((SWA_Q_W, D_MODEL)), whole((D_MODEL, D_MODEL)),
            whole((1, GATE_W)), whole((1, D_MODEL)), whole((1, D_MODEL)),
        ],
        out_specs=pair,
        out_shape=jax.ShapeDtypeStruct((batch * seq, D_MODEL), F32),
        scratch_shapes=[
            pltpu.VMEM((SEQ_TILE, D_MODEL), BF16),
            pltpu.VMEM((SEQ_TILE, MX_COLS), BF16),
            pltpu.VMEM((SEQ_TILE, EW_COLS), F32),
            pltpu.VMEM((SEQ_TILE, MX_COLS), BF16),
            pltpu.VMEM((SEQ_TILE, EW_COLS), F32),
            pltpu.VMEM((_GlaTile.PAIRS, _GlaTile.PAIR_K, _GlaTile.PAIR_V), F32),
            pltpu.VMEM((SWA_BLOCK, SWA_KV_W), F32),
            pltpu.VMEM((SWA_BLOCK, SWA_KV_W), F32),
        ],
        compiler_params=pltpu.CompilerParams(
            dimension_semantics=("arbitrary",), vmem_limit_bytes=VMEM_LIMIT_BYTES),
        name="hybrid_layer",
    )(x2d, x2d, w_packed, w_gk2p, row(b_gk), row(gla_norm_g), _swa_bias_table(), sinks.astype(F32),
      w_o_gla.astype(BF16), w_o_swa.astype(BF16), w_out.astype(BF16), row(b_gate), row(ln_g), row(ln_b))
    return out.reshape(batch, seq, d_model).astype(x.dtype)
```

```python
import functools

import jax
import jax.numpy as jnp
from jax import lax
from jax.experimental import pallas as pl
from jax.experimental.pallas import tpu as pltpu

F32 = jnp.float32
BF16 = jnp.bfloat16

D_MODEL = 1024
DEPTH = 1
GLA_HEADS = 4
GLA_DK = 64
GLA_DV = 128
GLA_RANK = 16
GLA_CHUNK = 64
GLA_GATE_NORM = 16.0
SWA_HEADS = 8
SWA_KV_HEADS = 2
SWA_HD = 64
SWA_WINDOW = 128
SWA_BLOCK = 128
GLA_QK_W = GLA_HEADS * GLA_DK
GLA_V_W = GLA_HEADS * GLA_DV
SWA_Q_W = SWA_HEADS * SWA_HD
SWA_KV_W = SWA_KV_HEADS * SWA_HD
GATE_W = 2 * D_MODEL
DEEPNORM_ALPHA = (2.0 * DEPTH) ** 0.25
LN_EPS = 1e-5
RMS_EPS = 1e-6

LANES = 128
RANK_PAD = LANES
VMEM_LIMIT_BYTES = 56 * 1024 * 1024

MX_V_G = 0
MX_LR = MX_V_G + GLA_V_W
MX_Q_S = MX_LR + RANK_PAD
MX_COLS = MX_Q_S + SWA_Q_W
EW_Q_G = 0
EW_K_G = EW_Q_G + GLA_QK_W
EW_Z_G = EW_K_G + GLA_QK_W
EW_K_S = EW_Z_G + GLA_V_W
EW_V_S = EW_K_S + SWA_KV_W
EW_Z_S = EW_V_S + SWA_KV_W
EW_GATE = EW_Z_S + SWA_Q_W
EW_COLS = EW_GATE + GATE_W
WH_Q_G = 0
WH_V_G = WH_Q_G + 2 * GLA_QK_W
WH_COLS = WH_V_G + GLA_V_W
WT_Z_G = 0
WT_Q_S = WT_Z_G + GLA_V_W
WT_K_S = WT_Q_S + SWA_Q_W
WT_Z_S = WT_K_S + 2 * SWA_KV_W
WT_GATE = WT_Z_S + SWA_Q_W
WT_COLS = WT_GATE + GATE_W

PROJ_COL_CHUNK = 512
SEQ_TILE = 256


def _sigmoid(v):
    return 1.0 / (1.0 + jnp.exp(-v))


def _silu(v):
    return v * _sigmoid(v)


def _log_sigmoid(v):
    return jnp.minimum(v, 0.0) - jnp.log1p(jnp.exp(-jnp.abs(v)))


def _split2(a):
    hi = a.astype(BF16)
    lo = (a - hi.astype(F32)).astype(BF16)
    return hi, lo


_dot = functools.partial(jnp.dot, preferred_element_type=F32)
_dot_nt = functools.partial(lax.dot_general, dimension_numbers=(((1,), (1,)), ((), ())), preferred_element_type=F32)
_dot_tn = functools.partial(lax.dot_general, dimension_numbers=(((0,), (0,)), ((), ())), preferred_element_type=F32)


def _projection_steps(x_ref, row0, w_refs, bgate_ref, xb_ref, mx_ref, ew_ref):
    wh_ref, wlr_ref, wt_ref = w_refs

    def gate_epilogue(c0):
        return lambda acc: _sigmoid(acc + bgate_ref[:, c0:c0 + PROJ_COL_CHUNK])

    groups = [
        (ew_ref, EW_Q_G, 2 * GLA_QK_W, wh_ref, WH_Q_G, None),
        (mx_ref, MX_V_G, GLA_V_W, wh_ref, WH_V_G, None),
        (mx_ref, MX_LR, RANK_PAD, wlr_ref, 0, None),
        (ew_ref, EW_Z_G, GLA_V_W, wt_ref, WT_Z_G, _silu),
        (mx_ref, MX_Q_S, SWA_Q_W, wt_ref, WT_Q_S, lambda acc: acc * (SWA_HD ** -0.5)),
        (ew_ref, EW_K_S, 2 * SWA_KV_W, wt_ref, WT_K_S, None),
        (ew_ref, EW_Z_S, SWA_Q_W, wt_ref, WT_Z_S, _silu),
    ] + [(ew_ref, EW_GATE + c0, PROJ_COL_CHUNK, wt_ref, WT_GATE + c0, gate_epilogue(c0))
         for c0 in range(0, GATE_W, PROJ_COL_CHUNK)]

    def cast_x():
        xb_ref[...] = x_ref[row0:row0 + SEQ_TILE, :].astype(BF16)

    def make(dst, c0, width, w_ref, w0, epilogue):
        def step():
            acc = _dot(xb_ref[...], w_ref[:, w0:w0 + width])
            if epilogue is not None:
                acc = epilogue(acc)
            dst[:, c0:c0 + width] = acc.astype(dst.dtype)
        return step

    return [cast_x] + [make(*g) for g in groups]


class _Interleaver:
    def __init__(self, steps):
        self._steps = list(steps)

    def __call__(self, n=1):
        for _ in range(min(n, len(self._steps))):
            self._steps.pop(0)()

    def flush(self):
        self(len(self._steps))


class _GlaTile:
    PAIRS = GLA_HEADS // 2
    PAIR_K = 2 * GLA_DK
    PAIR_V = 2 * GLA_DV

    def __init__(self, mx_ref, ew_ref, wgk_ref, bgk_ref, ng_ref, state_ref):
        self.mx, self.ew = mx_ref, ew_ref
        self.wgk, self.bgk, self.ng, self.state = wgk_ref, bgk_ref, ng_ref, state_ref
        self.ts = mx_ref.shape[0]
        self.n_chunks = self.ts // GLA_CHUNK

    def gate_logits(self):
        self.gk = _dot(self.mx[:, MX_LR:MX_LR + RANK_PAD], self.wgk[...]) + self.bgk[...]

    def cumsum(self):
        ts = self.ts
        log_g = _log_sigmoid(self.gk) * (1.0 / GLA_GATE_NORM)
        row = lax.broadcasted_iota(jnp.int32, (ts, ts), 0)
        col = lax.broadcasted_iota(jnp.int32, (ts, ts), 1)
        lower = jnp.where(((row // GLA_CHUNK) == (col // GLA_CHUNK)) & (col <= row), 1.0, 0.0).astype(BF16)
        lg_hi, lg_lo = _split2(log_g)
        self.g_cum = _dot(lower, lg_hi) + _dot(lower, lg_lo)

    def chunk_scores(self):
        r_k = lax.broadcasted_iota(jnp.int32, (GLA_QK_W, GLA_QK_W), 0)
        c_k = lax.broadcasted_iota(jnp.int32, (GLA_QK_W, GLA_QK_W), 1)
        k_head_mask = (r_k // GLA_CHUNK) == (c_k // GLA_DK)
        r_v = lax.broadcasted_iota(jnp.int32, (self.PAIR_K, self.PAIR_V), 0)
        c_v = lax.broadcasted_iota(jnp.int32, (self.PAIR_K, self.PAIR_V), 1)
        self.pair_mask = (r_v // GLA_DK) == (c_v // GLA_DV)
        g_cum_t = self.g_cum.T
        self.q_dec, self.scores, self.kv, self.decay = [], [], [], []
        for c in range(self.n_chunks):
            r0, r1 = c * GLA_CHUNK, (c + 1) * GLA_CHUNK
            g_c = self.g_cum[r0:r1]
            g_rev = self.g_cum[r1 - 1:r1] - g_c
            q_c = self.ew[r0:r1, EW_Q_G:EW_Q_G + GLA_QK_W]
            k_c = self.ew[r0:r1, EW_K_G:EW_K_G + GLA_QK_W]
            q_dec = (q_c * ((GLA_DK ** -0.5) * jnp.exp(g_c))).astype(BF16)
            k_inv = (k_c * jnp.exp(-g_c)).astype(BF16)
            k_end = (k_c * jnp.exp(g_rev)).astype(BF16)
            k_bd = jnp.where(k_head_mask, jnp.concatenate([k_inv] * GLA_HEADS, axis=0), jnp.zeros((), BF16))
            self.q_dec.append(q_dec)
            self.scores.append(_dot_nt(q_dec, k_bd))
            self.kv.append([
                _dot_tn(k_end[:, p * self.PAIR_K:(p + 1) * self.PAIR_K],
                        self.mx[r0:r1, MX_V_G + p * self.PAIR_V:MX_V_G + (p + 1) * self.PAIR_V])
                for p in range(self.PAIRS)])
            self.decay.append([jnp.exp(g_cum_t[p * self.PAIR_K:(p + 1) * self.PAIR_K, r1 - 1:r1])
                               for p in range(self.PAIRS)])

    def chunk_outputs(self):
        r_a = lax.broadcasted_iota(jnp.int32, (GLA_CHUNK, GLA_QK_W), 0)
        c_a = lax.broadcasted_iota(jnp.int32, (GLA_CHUNK, GLA_QK_W), 1)
        causal = (c_a % GLA_CHUNK) <= r_a
        states = [self.state[p] for p in range(self.PAIRS)]
        o_chunks = []
        for c in range(self.n_chunks):
            r0, r1 = c * GLA_CHUNK, (c + 1) * GLA_CHUNK
            a = jnp.where(causal, self.scores[c], 0.0).astype(BF16)
            o_pairs = []
            for p in range(self.PAIRS):
                k0, k1 = p * self.PAIR_K, (p + 1) * self.PAIR_K
                v_p = self.mx[r0:r1, MX_V_G + p * self.PAIR_V:MX_V_G + (p + 1) * self.PAIR_V]
                v_bd = jnp.where(self.pair_mask, jnp.concatenate([v_p, v_p], axis=0), jnp.zeros((), BF16))
                o_pairs.append(_dot(a[:, k0:k1], v_bd) + _dot(self.q_dec[c][:, k0:k1], states[p].astype(BF16)))
                states[p] = states[p] * self.decay[c][p] + jnp.where(self.pair_mask, self.kv[c][p], 0.0)
            o_chunks.append(jnp.concatenate(o_pairs, axis=1))
        for p in range(self.PAIRS):
            self.state[p] = states[p]
        self.o = jnp.concatenate(o_chunks, axis=0)

    def finish(self):
        normed = []
        for hd in range(GLA_HEADS):
            o_h = self.o[:, hd * GLA_DV:(hd + 1) * GLA_DV]
            ms = jnp.mean(o_h * o_h, axis=-1, keepdims=True)
            normed.append(o_h * lax.rsqrt(ms + RMS_EPS) * self.ng[...])
        o_n = jnp.concatenate(normed, axis=1)
        return (o_n * self.ew[:, EW_Z_G:EW_Z_G + GLA_V_W]).astype(BF16)


def _swa_bias_table():
    w = SWA_BLOCK
    slopes = 2.0 ** (-8.0 * jnp.arange(1, SWA_HEADS + 1, dtype=F32) / SWA_HEADS)
    q_idx = jnp.arange(w)[None, :]
    k_idx = jnp.arange(2 * w)[:, None]
    dist = q_idx - (k_idx - w)
    valid = (dist >= 0) & (dist < SWA_WINDOW)
    bias = -slopes[:, None, None] * dist.astype(F32)[None]
    rest = jnp.where(valid[None], bias, -jnp.inf)
    first = jnp.where((valid & (k_idx >= w))[None], bias, -jnp.inf)
    return jnp.stack([first, rest]).astype(F32)


class _SwaTile:
    GROUP = SWA_HEADS // SWA_KV_HEADS
    PAIRS = GROUP // 2

    def __init__(self, mx_ref, ew_ref, bias_ref, sink_ref, kprev_ref, vtprev_ref, maybe_first):
        self.mx, self.ew = mx_ref, ew_ref
        self.bias, self.sink, self.kprev, self.vtprev = bias_ref, sink_ref, kprev_ref, vtprev_ref
        self.maybe_first = maybe_first
        self.ts = mx_ref.shape[0]
        self.n_blocks = self.ts // SWA_BLOCK
        self.s, self.vt, self.outs_t = {}, {}, {}

    def _k(self, n):
        w = SWA_BLOCK
        return self.ew[n * w:(n + 1) * w, EW_K_S:EW_K_S + SWA_KV_W]

    def _vt(self, n):
        if n not in self.vt:
            w = SWA_BLOCK
            self.vt[n] = self.vtprev[...] if n < 0 else self.ew[n * w:(n + 1) * w, EW_V_S:EW_V_S + SWA_KV_W].T
        return self.vt[n]

    def scores(self, n):
        w = SWA_BLOCK
        r0, r1 = n * w, (n + 1) * w
        k_cat = jnp.concatenate([self.kprev[...] if n == 0 else self._k(n - 1), self._k(n)], axis=0)
        k_swap = pltpu.roll(k_cat, SWA_HD, 1)
        low = lax.broadcasted_iota(jnp.int32, (2 * w, SWA_KV_W), 1) < SWA_HD
        for g in range(SWA_KV_HEADS):
            k_low, k_high = (k_cat, k_swap) if g == 0 else (k_swap, k_cat)
            k_sides = (jnp.where(low, k_low, 0.0).astype(BF16), jnp.where(low, 0.0, k_high).astype(BF16))
            q_st = jnp.concatenate(
                [self.mx[r0:r1, MX_Q_S + (g * self.PAIRS + p) * LANES:MX_Q_S + (g * self.PAIRS + p + 1) * LANES]
                 for p in range(self.PAIRS)], axis=0)
            for parity in range(2):
                self.s[n, g, parity] = _dot_nt(k_sides[parity], q_st)

    def attend(self, n, g):
        w = SWA_BLOCK
        if n == 0 and self.maybe_first is not None:
            bias_idx = jnp.where(self.maybe_first, 0, 1)
        else:
            bias_idx = 1
        probs = []
        for p in range(self.PAIRS):
            for parity in range(2):
                head = g * self.GROUP + 2 * p + parity
                sc = self.s[n, g, parity][:, p * w:(p + 1) * w] + self.bias[bias_idx, head]
                sink = self.sink[head]
                m = jnp.maximum(jnp.max(sc, axis=0, keepdims=True), sink)
                e = jnp.exp(sc - m)
                denom = jnp.sum(e, axis=0, keepdims=True) + jnp.exp(sink - m)
                probs.append((e * (1.0 / denom)).astype(BF16))
        vt = jnp.concatenate([self._vt(n - 1), self._vt(n)], axis=1)[g * SWA_HD:(g + 1) * SWA_HD]
        self.outs_t[n, g] = _dot(vt.astype(BF16), jnp.concatenate(probs, axis=1))

    def finish(self):
        w = SWA_BLOCK
        tiles = []
        for n in range(self.n_blocks):
            groups = []
            for g in range(SWA_KV_HEADS):
                for p in range(self.PAIRS):
                    two_heads_t = jnp.concatenate(
                        [self.outs_t[n, g][:, (2 * p + parity) * w:(2 * p + parity + 1) * w] for parity in range(2)],
                        axis=0)
                    groups.append(two_heads_t.T)
            o_s = jnp.concatenate(groups, axis=1)
            tiles.append((o_s * self.ew[n * w:(n + 1) * w, EW_Z_S:EW_Z_S + SWA_Q_W]).astype(BF16))
        self.kprev[...] = self._k(self.n_blocks - 1)
        self.vtprev[...] = self._vt(self.n_blocks - 1)
        return jnp.concatenate(tiles, axis=0)


def _layer_kernel(tiles_per_seq, x_ref, xnext_ref, wh_ref, wlr_ref, wt_ref, wgk_ref, bgk_ref, ng_ref, bias_ref,
                  sink_ref, wog_ref, wos_ref, wout_ref, bgate_ref, lng_ref, lnb_ref, o_ref,
                  xb_ref, mx0_ref, ew0_ref, mx1_ref, ew1_ref, state_ref, kprev_ref, vprev_ref):
    j = pl.program_id(0)
    first_tile = (2 * j) % tiles_per_seq == 0

    def projection(src_ref, row0, mx_ref, ew_ref):
        return _projection_steps(src_ref, row0, (wh_ref, wlr_ref, wt_ref), bgate_ref, xb_ref, mx_ref, ew_ref)

    @pl.when(j == 0)
    def _():
        _Interleaver(projection(x_ref, 0, mx0_ref, ew0_ref)).flush()

    @pl.when(first_tile)
    def _():
        state_ref[...] = jnp.zeros_like(state_ref)
        kprev_ref[...] = jnp.zeros_like(kprev_ref)
        vprev_ref[...] = jnp.zeros_like(vprev_ref)

    def stages(row0, mx_ref, ew_ref, maybe_first, next_projection):
        tick = _Interleaver(next_projection)
        gla = _GlaTile(mx_ref, ew_ref, wgk_ref, bgk_ref, ng_ref, state_ref)
        swa = _SwaTile(mx_ref, ew_ref, bias_ref, sink_ref, kprev_ref, vprev_ref, maybe_first)
        tick()
        swa.scores(0)
        gla.gate_logits()
        tick()
        swa.scores(1)
        tick()
        gla.cumsum()
        swa.attend(0, 0)
        tick(2)
        gla.chunk_scores()
        swa.attend(0, 1)
        tick()
        gla.chunk_outputs()
        swa.attend(1, 0)
        tick(2)
        swa.attend(1, 1)
        tick()
        a_gla = gla.finish()
        a_swa = swa.finish()
        y_gla = _dot(a_gla, wog_ref[...])
        y_swa = _dot(a_swa, wos_ref[...])
        tick()
        merged = (ew_ref[:, EW_GATE:EW_GATE + D_MODEL] * y_gla
                  + ew_ref[:, EW_GATE + D_MODEL:EW_GATE + GATE_W] * y_swa)
        tick()
        y = _dot(merged.astype(BF16), wout_ref[...])
        tick()
        r = DEEPNORM_ALPHA * x_ref[row0:row0 + SEQ_TILE, :] + y
        mu = jnp.mean(r, axis=-1, keepdims=True)
        cen = r - mu
        var = jnp.mean(cen * cen, axis=-1, keepdims=True)
        o_ref[row0:row0 + SEQ_TILE, :] = cen * lax.rsqrt(var + LN_EPS) * lng_ref[...] + lnb_ref[...]
        tick.flush()

    stages(0, mx0_ref, ew0_ref, first_tile, projection(x_ref, SEQ_TILE, mx1_ref, ew1_ref))
    stages(SEQ_TILE, mx1_ref, ew1_ref, None, projection(xnext_ref, 0, mx0_ref, ew0_ref))


def _split_w_in(w_in):
    lr = jnp.pad(w_in[:, WH_COLS:WH_COLS + GLA_RANK], ((0, 0), (0, RANK_PAD - GLA_RANK)))
    return w_in[:, :WH_COLS].astype(BF16), lr.astype(BF16), w_in[:, WH_COLS + GLA_RANK:].astype(BF16)


def kernel(x, w_in, w_gk2, b_gk, gla_norm_g, w_o_gla, sinks, w_o_swa, b_gate, w_out, ln_g, ln_b):
    batch, seq, d_model = x.shape
    assert d_model == D_MODEL and DEPTH == 1
    assert seq % (2 * SEQ_TILE) == 0 and SEQ_TILE % GLA_CHUNK == 0 and SEQ_TILE % SWA_BLOCK == 0
    tiles_per_seq = seq // SEQ_TILE
    n_tiles = batch * tiles_per_seq
    x2d = x.reshape(batch * seq, d_model)

    w_head, w_lr, w_tail = _split_w_in(w_in)
    assert w_tail.shape[1] == WT_COLS
    w_gk2p = jnp.pad(w_gk2, ((0, RANK_PAD - GLA_RANK), (0, 0))).astype(BF16)
    row = lambda v: v.reshape(1, -1).astype(F32)

    pair = pl.BlockSpec((2 * SEQ_TILE, D_MODEL), lambda j: (j, 0))
    nxt = pl.BlockSpec((SEQ_TILE, D_MODEL), lambda j: (jnp.minimum(2 * j + 2, n_tiles - 1), 0))
    whole = lambda shape: pl.BlockSpec(shape, lambda j: (0,) * len(shape), pipeline_mode=pl.Buffered(1))
    out = pl.pallas_call(
        functools.partial(_layer_kernel, tiles_per_seq),
        grid=(n_tiles // 2,),
        in_specs=[
            pair, nxt,
            whole((D_MODEL, WH_COLS)), whole((D_MODEL, RANK_PAD)), whole((D_MODEL, WT_COLS)),
            whole((RANK_PAD, GLA_QK_W)), whole((1, GLA_QK_W)), whole((1, GLA_DV)),
            whole((2, SWA_HEADS, 2 * SWA_BLOCK, SWA_BLOCK)),
            pl.BlockSpec(memory_space=pltpu.SMEM),
            whole((GLA_V_W, D_MODEL)), whole((SWA_Q_W, D_MODEL)), whole((D_MODEL, D_MODEL)),
            whole((1, GATE_W)), whole((1, D_MODEL)), whole((1, D_MODEL)),
        ],
        out_specs=pair,
        out_shape=jax.ShapeDtypeStruct((batch * seq, D_MODEL), F32),
        scratch_shapes=[
            pltpu.VMEM((SEQ_TILE, D_MODEL), BF16),
            pltpu.VMEM((SEQ_TILE, MX_COLS), BF16),
            pltpu.VMEM((SEQ_TILE, EW_COLS), F32),
            pltpu.VMEM((SEQ_TILE, MX_COLS), BF16),
            pltpu.VMEM((SEQ_TILE, EW_COLS), F32),
            pltpu.VMEM((_GlaTile.PAIRS, _GlaTile.PAIR_K, _GlaTile.PAIR_V), F32),
            pltpu.VMEM((SWA_BLOCK, SWA_KV_W), F32),
            pltpu.VMEM((SWA_BLOCK, SWA_KV_W), F32),
        ],
        compiler_params=pltpu.CompilerParams(
            dimension_semantics=("arbitrary",), vmem_limit_bytes=VMEM_LIMIT_BYTES),
        name="hybrid_layer",
    )(x2d, x2d, w_head, w_lr, w_tail, w_gk2p, row(b_gk), row(gla_norm_g), _swa_bias_table(), sinks.astype(F32),
      w_o_gla.astype(BF16), w_o_swa.astype(BF16), w_out.astype(BF16), row(b_gate), row(ln_g), row(ln_b))
    return out.reshape(batch, seq, d_model).astype(x.dtype)
```
